```python
import jax, jax.numpy as jnp
from jax import lax
import numpy as np

D_MODEL = 1024
BATCH = 32
SEQ = 2048
DEPTH = 1
DEC_BATCH = 16
DEC_SEQ = 16
PAST_LEN = 4096

CHUNK = 64
EPS = 1e-6
NEG_INF = -1e30
H_A = 8
NOPE_DIM = 64
ROPE_DIM = 32
V_DIM = 64
Q_RANK = 256
KV_RANK = 256
ROPE_THETA = 10000.0
MLA_SCALE = (NOPE_DIM + ROPE_DIM) ** -0.5
Q_BLOCK = 128
H_B = 8
HD_B = 64
N_PREV_CHUNKS = 8
BAND_PAST = N_PREV_CHUNKS * CHUNK
BAND = (N_PREV_CHUNKS + 1) * CHUNK
REL_CLIP = 256
N_REL = 2 * REL_CLIP + 1
BAND_SCALE = HD_B ** -0.5
MIX_A = H_A * V_DIM
MIX_B = H_B * HD_B
MIX_WIDTH = MIX_A + MIX_B
IN_COLS = Q_RANK + KV_RANK + ROPE_DIM + 3 * MIX_B
SPLITS = (Q_RANK, Q_RANK + KV_RANK, Q_RANK + KV_RANK + ROPE_DIM)
D_FF = -(-8 * D_MODEL // (3 * 256)) * 256

kernel_name = "hymba_mla_chunkband_stream_step"


def rmsnorm(x, g):
    xf = x.astype(jnp.float32)
    xf = xf * lax.rsqrt(jnp.mean(xf * xf, axis=-1, keepdims=True) + EPS)
    return xf.astype(x.dtype) * g


def rope(x, pos):
    inv = ROPE_THETA ** (-jnp.arange(0, ROPE_DIM, 2, dtype=jnp.float32) / ROPE_DIM)
    ang = pos.astype(jnp.float32)[:, None] * inv[None, :]
    cos = jnp.cos(ang)[None, :, None, :]
    sin = jnp.sin(ang)[None, :, None, :]
    x1, x2 = jnp.split(x.astype(jnp.float32), 2, axis=-1)
    return jnp.concatenate([x1 * cos - x2 * sin, x1 * sin + x2 * cos], axis=-1).astype(x.dtype)


def head_group_inputs(xn, pos, w_in, g_q, w_uq, g_kv):
    b, s, _ = xn.shape
    h = xn @ w_in
    c_q, c_kv, k_r, qkv = jnp.split(h, SPLITS, axis=-1)
    q = (rmsnorm(c_q, g_q) @ w_uq).reshape(b, s, H_A, NOPE_DIM + ROPE_DIM)
    q_nope = q[..., :NOPE_DIM]
    q_rope = rope(q[..., NOPE_DIM:], pos)
    c_kv = rmsnorm(c_kv, g_kv)
    k_rope = rope(k_r[:, :, None, :], pos)[:, :, 0, :]
    q_b, k_b, v_b = [t.reshape(b, s, H_B, HD_B) for t in jnp.split(qkv, 3, axis=-1)]
    return q_nope, q_rope, c_kv, k_rope, q_b, k_b, v_b


def mla_core(q_nope, q_rope, k_nope, k_rope, v, mask):
    s = (jnp.einsum('bqhd,bkhd->bhqk', q_nope, k_nope)
         + jnp.einsum('bqhr,bkr->bhqk', q_rope, k_rope)).astype(jnp.float32) * MLA_SCALE
    if mask is not None:
        s = jnp.where(mask, s, NEG_INF)
    p = jax.nn.softmax(s, axis=-1).astype(v.dtype)
    return jnp.einsum('bhqk,bkhd->bqhd', p, v)


def mla_prompt(q_nope, q_rope, c_kv, k_rope, w_uk, w_uv):
    b, s = q_nope.shape[:2]
    k_nope = jnp.einsum('bkc,chd->bkhd', c_kv, w_uk)
    v = jnp.einsum('bkc,chd->bkhd', c_kv, w_uv)
    nb = s // Q_BLOCK
    key_chunk = jnp.arange(s) // CHUNK

    def one_block(args):
        qn, qr, blk = args
        q_chunk = (blk * Q_BLOCK + jnp.arange(Q_BLOCK)) // CHUNK
        mask = key_chunk[None, :] <= q_chunk[:, None]
        return mla_core(qn, qr, k_nope, k_rope, v, mask)

    qn = q_nope.reshape(b, nb, Q_BLOCK, H_A, NOPE_DIM).swapaxes(0, 1)
    qr = q_rope.reshape(b, nb, Q_BLOCK, H_A, ROPE_DIM).swapaxes(0, 1)
    out = lax.map(one_block, (qn, qr, jnp.arange(nb)))
    return out.swapaxes(0, 1).reshape(b, s, MIX_A)


def mla_sample(q_nope, q_rope, c_kv, k_rope, cache_ckv, cache_kr, w_uk, w_uv):
    b, s = q_nope.shape[:2]
    ckv_all = jnp.concatenate([cache_ckv, c_kv], axis=1)
    kr_all = jnp.concatenate([cache_kr, k_rope], axis=1)
    k_nope = jnp.einsum('bkc,chd->bkhd', ckv_all, w_uk)
    v = jnp.einsum('bkc,chd->bkhd', ckv_all, w_uv)
    return mla_core(q_nope, q_rope, k_nope, kr_all, v, None).reshape(b, s, MIX_A)


def band_core(q, k, v, dist, valid, rel_bias):
    bias = rel_bias[:, jnp.clip(dist, -REL_CLIP, REL_CLIP) + REL_CLIP]
    s = jnp.einsum('...qhd,...khd->...hqk', q, k).astype(jnp.float32) * BAND_SCALE \
        + bias.astype(jnp.float32)
    if valid is not None:
        s = jnp.where(valid, s, NEG_INF)
    p = jax.nn.softmax(s, axis=-1).astype(v.dtype)
    return jnp.einsum('...hqk,...khd->...qhd', p, v)


def band_prompt(q, k, v, rel_bias):
    b, s = q.shape[:2]
    nc = s // CHUNK
    qc = q.reshape(b, nc, CHUNK, H_B, HD_B)
    pad = jnp.zeros((b, BAND_PAST, H_B, HD_B), k.dtype)
    kp = jnp.concatenate([pad, k], axis=1).reshape(b, nc + N_PREV_CHUNKS, CHUNK, H_B, HD_B)
    vp = jnp.concatenate([pad, v], axis=1).reshape(b, nc + N_PREV_CHUNKS, CHUNK, H_B, HD_B)
    idx = jnp.arange(nc)[:, None] + jnp.arange(N_PREV_CHUNKS + 1)[None, :]
    kb = kp[:, idx].reshape(b, nc, BAND, H_B, HD_B)
    vb = vp[:, idx].reshape(b, nc, BAND, H_B, HD_B)
    dist = jnp.arange(CHUNK)[:, None] - (jnp.arange(BAND) - BAND_PAST)[None, :]
    key_chunk = jnp.arange(nc)[:, None] - N_PREV_CHUNKS + (jnp.arange(BAND) // CHUNK)[None, :]
    valid = (key_chunk >= 0)[:, None, None, :]
    out = band_core(qc, kb, vb, dist, valid, rel_bias)
    return out.reshape(b, s, MIX_B)


def band_sample(q, k, v, cache_k, cache_v, rel_bias):
    b, t = q.shape[:2]
    n_c = cache_k.shape[1]
    k_all = jnp.concatenate([cache_k, k], axis=1)
    v_all = jnp.concatenate([cache_v, v], axis=1)
    kpos = jnp.concatenate([jnp.arange(n_c) - n_c, jnp.arange(t)])
    dist = jnp.arange(t)[:, None] - kpos[None, :]
    return band_core(q, k_all, v_all, dist, None, rel_bias).reshape(b, t, MIX_B)


def layer_forward(x, pos, cache, w_in, g_attn, g_q, w_uq, g_kv, w_uk, w_uv, rel_bias,
                  g_out_a, g_out_b, w_out, g_ffn, w_gate, w_up, w_down):
    xn = rmsnorm(x, g_attn)
    q_nope, q_rope, c_kv, k_rope, q_b, k_b, v_b = head_group_inputs(xn, pos, w_in, g_q, w_uq, g_kv)
    if cache is None:
        out_a = mla_prompt(q_nope, q_rope, c_kv, k_rope, w_uk, w_uv)
        out_b = band_prompt(q_b, k_b, v_b, rel_bias)
        n_keep = min(BAND_PAST, x.shape[1])
        new_state = (c_kv, k_rope, k_b[:, x.shape[1] - n_keep:], v_b[:, x.shape[1] - n_keep:])
    else:
        cache_ckv, cache_kr, cache_bk, cache_bv = cache
        out_a = mla_sample(q_nope, q_rope, c_kv, k_rope, cache_ckv, cache_kr, w_uk, w_uv)
        out_b = band_sample(q_b, k_b, v_b, cache_bk, cache_bv, rel_bias)
        new_state = (c_kv, k_rope, k_b, v_b)
    mix = jnp.concatenate([rmsnorm(out_a, g_out_a), rmsnorm(out_b, g_out_b)], axis=-1)
    x = x + mix @ w_out
    h = rmsnorm(x, g_ffn)
    x = x + (jax.nn.silu(h @ w_gate) * (h @ w_up)) @ w_down
    return x, new_state


def setup_inputs(seed: int = 0) -> dict:
    key = jax.random.key(seed)
    ks = jax.random.split(key, 24)
    f32 = jnp.float32

    def nrm(k, shape, scale=1.0):
        return jax.random.normal(k, shape, f32) * scale

    def gain(k, shape):
        return 1.0 + 0.02 * jax.random.normal(k, shape, f32)

    n_band = min(BAND_PAST, PAST_LEN)
    return {
        "x_prompt": nrm(ks[0], (BATCH, SEQ, D_MODEL)),
        "x_sample": nrm(ks[1], (DEC_BATCH, DEC_SEQ, D_MODEL)),
        "cache_mla_ckv": nrm(ks[2], (DEPTH, DEC_BATCH, PAST_LEN, KV_RANK)),
        "cache_mla_krope": nrm(ks[3], (DEPTH, DEC_BATCH, PAST_LEN, ROPE_DIM)),
        "cache_band_k": nrm(ks[4], (DEPTH, DEC_BATCH, n_band, H_B, HD_B)),
        "cache_band_v": nrm(ks[5], (DEPTH, DEC_BATCH, n_band, H_B, HD_B)),
        "w_in": nrm(ks[6], (DEPTH, D_MODEL, IN_COLS), D_MODEL ** -0.5),
        "g_attn": gain(ks[7], (DEPTH, D_MODEL)),
        "g_q": gain(ks[8], (DEPTH, Q_RANK)),
        "w_uq": nrm(ks[9], (DEPTH, Q_RANK, H_A * (NOPE_DIM + ROPE_DIM)), Q_RANK ** -0.5),
        "g_kv": gain(ks[10], (DEPTH, KV_RANK)),
        "w_uk": nrm(ks[11], (DEPTH, KV_RANK, H_A, NOPE_DIM), KV_RANK ** -0.5),
        "w_uv": nrm(ks[12], (DEPTH, KV_RANK, H_A, V_DIM), KV_RANK ** -0.5),
        "rel_bias": nrm(ks[13], (DEPTH, H_B, N_REL), 0.5),
        "g_out_a": gain(ks[14], (DEPTH, MIX_A)),
        "g_out_b": gain(ks[15], (DEPTH, MIX_B)),
        "w_out": nrm(ks[16], (DEPTH, MIX_WIDTH, D_MODEL), MIX_WIDTH ** -0.5),
        "g_ffn": gain(ks[17], (DEPTH, D_MODEL)),
        "w_gate": nrm(ks[18], (DEPTH, D_MODEL, D_FF), D_MODEL ** -0.5),
        "w_up": nrm(ks[19], (DEPTH, D_MODEL, D_FF), D_MODEL ** -0.5),
        "w_down": nrm(ks[20], (DEPTH, D_FF, D_MODEL), D_FF ** -0.5),
        "g_final": gain(ks[21], (D_MODEL,)),
    }


def reference(x_prompt, x_sample, cache_mla_ckv, cache_mla_krope, cache_band_k, cache_band_v,
              w_in, g_attn, g_q, w_uq, g_kv, w_uk, w_uv, rel_bias, g_out_a, g_out_b, w_out,
              g_ffn, w_gate, w_up, w_down, g_final):
    past_len = cache_mla_ckv.shape[2]
    pos_p = jnp.arange(x_prompt.shape[1], dtype=jnp.int32)
    pos_s = past_len + jnp.arange(x_sample.shape[1], dtype=jnp.int32)
    yp, ys = x_prompt, x_sample
    st_p, st_s = [], []
    for l in range(DEPTH):
        w = (w_in[l], g_attn[l], g_q[l], w_uq[l], g_kv[l], w_uk[l], w_uv[l], rel_bias[l],
             g_out_a[l], g_out_b[l], w_out[l], g_ffn[l], w_gate[l], w_up[l], w_down[l])
        yp, sp = layer_forward(yp, pos_p, None, *w)
        ys, ss = layer_forward(ys, pos_s, (cache_mla_ckv[l], cache_mla_krope[l],
                                           cache_band_k[l], cache_band_v[l]), *w)
        st_p.append(sp)
        st_s.append(ss)
    y_prompt = rmsnorm(yp, g_final)
    y_sample = rmsnorm(ys, g_final)
    new_ckv_prompt = jnp.stack([s[0] for s in st_p])
    new_kr_prompt = jnp.stack([s[1] for s in st_p])
    new_bk_prompt = jnp.stack([s[2] for s in st_p])
    new_bv_prompt = jnp.stack([s[3] for s in st_p])
    new_ckv_sample = jnp.stack([s[0] for s in st_s])
    new_kr_sample = jnp.stack([s[1] for s in st_s])
    new_bk_sample = jnp.stack([s[2] for s in st_s])
    new_bv_sample = jnp.stack([s[3] for s in st_s])
    return (y_prompt, y_sample, new_ckv_prompt, new_kr_prompt, new_bk_prompt, new_bv_prompt,
            new_ckv_sample, new_kr_sample, new_bk_sample, new_bv_sample)
```

```python
import functools

import jax
import jax.numpy as jnp
from jax import lax
from jax.experimental import pallas as pl
from jax.experimental.pallas import tpu as pltpu

CHUNK = 64
EPS = 1e-6
NEG_INF = -1e30
H_A = 8
NOPE_DIM = 64
ROPE_DIM = 32
V_DIM = 64
Q_RANK = 256
KV_RANK = 256
ROPE_THETA = 10000.0
MLA_SCALE = (NOPE_DIM + ROPE_DIM) ** -0.5
H_B = 8
HD_B = 64
N_PREV_CHUNKS = 8
BAND_PAST = N_PREV_CHUNKS * CHUNK
REL_CLIP = 256
BAND_SCALE = HD_B ** -0.5
MIX_A = H_A * V_DIM
MIX_B = H_B * HD_B

LANES = 128
MXU_DIM = 256
VMEM_LIMIT_BYTES = 56 * 1024 * 1024

ATT_TILE = 4 * CHUNK
BAND_TILES = N_PREV_CHUNKS * CHUNK // ATT_TILE + 1
PAIR = 2 * V_DIM
ROPE_PER_BLOCK = LANES // ROPE_DIM

F32 = jnp.float32
BF16 = jnp.bfloat16


def _dot(a, b):
    return jnp.dot(a, b, preferred_element_type=F32)


def _dot_nt(a, b):
    return lax.dot_general(a, b, (((1,), (1,)), ((), ())), preferred_element_type=F32)


def _rms(x, g):
    return x * lax.rsqrt(jnp.mean(x * x, axis=-1, keepdims=True) + EPS) * g


def _lane_iota(width=LANES):
    return lax.broadcasted_iota(jnp.int32, (1, width), 1)


def _compiler_params(semantics):
    return pltpu.CompilerParams(dimension_semantics=semantics, vmem_limit_bytes=VMEM_LIMIT_BYTES)


def _const_spec(shape):
    nd = len(shape)
    return pl.BlockSpec(shape, lambda *_: (0,) * nd, pipeline_mode=pl.Buffered(1))


def _bias_kernel(g_ref, t_ref):
    rows, width = t_ref.shape[1], g_ref.shape[2]
    x = jnp.broadcast_to(g_ref[0], (rows, width))
    x = pltpu.roll(x, 0, 1, stride=1, stride_axis=0)
    x = x[:, width - t_ref.shape[2]:]
    qc = lax.broadcasted_iota(jnp.int32, x.shape, 0) // CHUNK
    kc = lax.broadcasted_iota(jnp.int32, x.shape, 1) // CHUNK
    visible = jnp.logical_and(kc >= qc, kc <= qc + N_PREV_CHUNKS)
    t_ref[0] = jnp.where(visible, x, NEG_INF)


def _bias_table(rel_bias):
    n_keys = BAND_TILES * ATT_TILE
    width = n_keys + ATT_TILE
    n_const = width - 2 * REL_CLIP + 1
    g = jnp.concatenate(
        [jnp.broadcast_to(rel_bias[:, 2 * REL_CLIP:], (H_B, n_const)),
         rel_bias[:, 2 * REL_CLIP - 1:0:-1]], axis=1)
    g = g.reshape(H_B, 1, width)
    return pl.pallas_call(
        _bias_kernel,
        grid=(H_B,),
        in_specs=[pl.BlockSpec((1, 1, width), lambda h: (h, 0, 0))],
        out_specs=pl.BlockSpec((1, ATT_TILE, n_keys), lambda h: (h, 0, 0)),
        out_shape=jax.ShapeDtypeStruct((H_B, ATT_TILE, n_keys), F32),
        compiler_params=_compiler_params(("arbitrary",)),
        name="bias_table",
    )(g)


def _proj_kernel(x_ref, tab_ref, ga_ref, gq_ref, gkv_ref, w1_ref, wq2_ref, wkv2_ref,
                 qn_ref, qr_ref, kn_ref, krt_ref, v_ref, qb_ref, kb_ref, vb_ref,
                 ckv_ref, kr_ref, bk_ref, bv_ref, *, tiles_per_seq, keep_from_tile):
    xn = _rms(x_ref[...], ga_ref[...]).astype(BF16)
    cos = tab_ref[:, 0:LANES]
    sin = tab_ref[:, LANES:2 * LANES]

    c_q = _dot(xn, w1_ref[:, 0:Q_RANK])
    cqn = _rms(c_q, gq_ref[...]).astype(BF16)
    q2 = _dot(cqn, wq2_ref[...])
    n_nope = H_A * NOPE_DIM
    n_rope = H_A * ROPE_DIM
    qn_ref[...] = (q2[:, 0:n_nope] * MLA_SCALE).astype(BF16)
    cos2 = jnp.concatenate([cos] * (n_rope // LANES), axis=1)
    sin2 = jnp.concatenate([sin] * (n_rope // LANES), axis=1)
    q_rope = q2[:, n_nope:n_nope + n_rope] * cos2 + q2[:, n_nope + n_rope:] * sin2
    qr_ref[...] = (q_rope * MLA_SCALE).astype(BF16)

    o = Q_RANK
    c_kv = _dot(xn, w1_ref[:, o:o + KV_RANK])
    ckvn = _rms(c_kv, gkv_ref[...])
    ckv_ref[...] = ckvn
    kv2 = _dot(ckvn.astype(BF16), wkv2_ref[...])
    kn_ref[...] = kv2[:, 0:n_nope].astype(BF16)
    v_ref[...] = kv2[:, n_nope:].astype(BF16)

    o += KV_RANK
    krr = _dot(xn, w1_ref[:, o:o + 2 * LANES])
    k_rope = krr[:, 0:LANES] * cos + krr[:, LANES:] * sin
    krt_ref[...] = k_rope.astype(BF16)
    kr_ref[...] = k_rope[:, 0:ROPE_DIM]

    o += 2 * LANES
    qkv = _dot(xn, w1_ref[:, o:o + 3 * MIX_B])
    qb_ref[...] = qkv[:, 0:MIX_B].astype(BF16)
    kb = qkv[:, MIX_B:2 * MIX_B]
    vb = qkv[:, 2 * MIX_B:]
    kb_ref[...] = kb.astype(BF16)
    vb_ref[...] = vb.astype(BF16)

    @pl.when(pl.program_id(0) % tiles_per_seq >= keep_from_tile)
    def _():
        bk_ref[0] = kb
        bv_ref[0] = vb


def _project(x2d, tab, weights, *, seq_len, n_keep, tm):
    n, d = x2d.shape
    g_attn, g_q, g_kv, w1, wq2, wkv2 = weights
    assert n % seq_len == 0 and seq_len % tm == 0 and n_keep % tm == 0
    tiles_per_seq = seq_len // tm
    keep_from_tile = (seq_len - n_keep) // tm
    n_seq = n // seq_len
    keep_tiles = n_keep // tm

    row = lambda width: pl.BlockSpec((tm, width), lambda i: (i, 0))
    keep_spec = pl.BlockSpec(
        (1, tm, MIX_B),
        lambda i: (i // tiles_per_seq, jnp.maximum(i % tiles_per_seq - keep_from_tile, 0), 0))
    bf = lambda width: jax.ShapeDtypeStruct((n, width), BF16)
    out_shape = (
        bf(H_A * NOPE_DIM), bf(H_A * ROPE_DIM), bf(H_A * NOPE_DIM), bf(LANES), bf(MIX_A),
        bf(MIX_B), bf(MIX_B), bf(MIX_B),
        jax.ShapeDtypeStruct((n, KV_RANK), F32), jax.ShapeDtypeStruct((n, ROPE_DIM), F32),
        jax.ShapeDtypeStruct((n_seq, keep_tiles * tm, MIX_B), F32),
        jax.ShapeDtypeStruct((n_seq, keep_tiles * tm, MIX_B), F32),
    )
    out_specs = (
        row(H_A * NOPE_DIM), row(H_A * ROPE_DIM), row(H_A * NOPE_DIM), row(LANES), row(MIX_A),
        row(MIX_B), row(MIX_B), row(MIX_B), row(KV_RANK), row(ROPE_DIM), keep_spec, keep_spec,
    )
    in_specs = [
        row(d),
        pl.BlockSpec((tm, 2 * LANES), lambda i: (i % tiles_per_seq, 0)),
        _const_spec(g_attn.shape), _const_spec(g_q.shape), _const_spec(g_kv.shape),
        _const_spec(w1.shape), _const_spec(wq2.shape), _const_spec(wkv2.shape),
    ]
    return pl.pallas_call(
        functools.partial(_proj_kernel, tiles_per_seq=tiles_per_seq, keep_from_tile=keep_from_tile),
        grid=(n // tm,),
        in_specs=in_specs,
        out_specs=out_specs,
        out_shape=out_shape,
        compiler_params=_compiler_params(("arbitrary",)),
        name="projection",
    )(x2d, tab, g_attn, g_q, g_kv, w1, wq2, wkv2)


def _flash_update(q, k2, v_tile, m_ref, l_ref, acc_ref, hh, mask):
    s = _dot_nt(q, k2)
    if mask is not None:
        s = jnp.where(mask, s, NEG_INF)
    m_prev = m_ref[hh]
    m_new = jnp.maximum(m_prev, jnp.max(s, axis=1, keepdims=True))
    p = jnp.exp(s - jnp.concatenate([m_new] * (s.shape[1] // LANES), axis=1))
    alpha = jnp.exp(m_prev - m_new)
    l_ref[hh] = alpha * l_ref[hh] + jnp.sum(p, axis=1, keepdims=True)
    acc_ref[hh] = acc_ref[hh] * alpha + _dot(p.astype(BF16), v_tile)
    m_ref[hh] = m_new


def _mla_kernel(qn_ref, qr_ref, kn_ref, krt_ref, v_ref, o_ref,
                ve_ref, vo_ref, m_ref, l_ref, acc_ref, *, n_tiles):
    t = ATT_TILE
    lane = _lane_iota()
    low = lane < V_DIM
    v = v_ref[...]
    zero = jnp.zeros_like(v)
    ve_ref[...] = jnp.where(low, v, zero)
    vo_ref[...] = jnp.where(low, zero, v)
    v_refs = (ve_ref, vo_ref)
    rope_base = (pl.program_id(1) % (ROPE_PER_BLOCK // 2)) * 2 * ROPE_DIM
    row_chunk = lax.broadcasted_iota(jnp.int32, (t, t), 0) // CHUNK
    col_chunk = lax.broadcasted_iota(jnp.int32, (t, t), 1) // CHUNK
    causal = col_chunk <= row_chunk

    def q_tile(qi, carry):
        r0 = pl.multiple_of(qi * t, t)
        qn = qn_ref[pl.ds(r0, t), :]
        qr = qr_ref[pl.ds(r0, t), :]
        q_heads = []
        for hh in range(2):
            nope_mask = low if hh == 0 else jnp.logical_not(low)
            lo = rope_base + hh * ROPE_DIM
            rope_mask = jnp.logical_and(lane >= lo, lane < lo + ROPE_DIM)
            q_heads.append(jnp.concatenate(
                [jnp.where(nope_mask, qn, jnp.zeros_like(qn)),
                 jnp.where(rope_mask, qr, jnp.zeros_like(qr))], axis=1))
        m_ref[...] = jnp.full(m_ref.shape, NEG_INF, F32)
        l_ref[...] = jnp.zeros(l_ref.shape, F32)
        acc_ref[...] = jnp.zeros(acc_ref.shape, F32)

        def kv_tile(kt, c):
            k0 = pl.multiple_of(kt * t, t)
            k2 = jnp.concatenate([kn_ref[pl.ds(k0, t), :], krt_ref[pl.ds(k0, t), :]], axis=1)
            for hh in range(2):
                _flash_update(q_heads[hh], k2, v_refs[hh][pl.ds(k0, t), :],
                              m_ref, l_ref, acc_ref, hh, None)
            return c

        lax.fori_loop(0, qi, kv_tile, 0)
        k2 = jnp.concatenate([kn_ref[pl.ds(r0, t), :], krt_ref[pl.ds(r0, t), :]], axis=1)
        for hh in range(2):
            _flash_update(q_heads[hh], k2, v_refs[hh][pl.ds(r0, t), :],
                          m_ref, l_ref, acc_ref, hh, causal)
        o_ref[pl.ds(r0, t), :] = acc_ref[0] / l_ref[0] + acc_ref[1] / l_ref[1]
        return carry

    lax.fori_loop(0, n_tiles, q_tile, 0)


def _mla_prompt(qn, qr, kn, krt, v, *, seq_len):
    n = qn.shape[0]
    assert seq_len % ATT_TILE == 0
    n_pairs = H_A // 2
    blk = lambda f: pl.BlockSpec((seq_len, LANES), f)
    return pl.pallas_call(
        functools.partial(_mla_kernel, n_tiles=seq_len // ATT_TILE),
        grid=(n // seq_len, n_pairs),
        in_specs=[
            blk(lambda b, j: (b, j)),
            blk(lambda b, j: (b, j // (ROPE_PER_BLOCK // 2))),
            blk(lambda b, j: (b, j)),
            blk(lambda b, j: (b, 0)),
            blk(lambda b, j: (b, j)),
        ],
        out_specs=blk(lambda b, j: (b, j)),
        out_shape=jax.ShapeDtypeStruct((n, MIX_A), F32),
        scratch_shapes=[
            pltpu.VMEM((seq_len, LANES), BF16), pltpu.VMEM((seq_len, LANES), BF16),
            pltpu.VMEM((2, ATT_TILE, LANES), F32), pltpu.VMEM((2, ATT_TILE, LANES), F32),
            pltpu.VMEM((2, ATT_TILE, LANES), F32),
        ],
        compiler_params=_compiler_params(("arbitrary", "arbitrary")),
        name="mla_prompt",
    )(qn, qr, kn, krt, v)


def _band_kernel(q_ref, k_ref, v_ref, t_ref, o_ref, ve_ref, vo_ref, *, n_tiles):
    t = ATT_TILE
    lane = _lane_iota()
    low = lane < HD_B
    v = v_ref[...]
    zero = jnp.zeros_like(v)
    ve_ref[...] = jnp.where(low, v, zero)
    vo_ref[...] = jnp.where(low, zero, v)
    v_refs = (ve_ref, vo_ref)

    def q_tile(g, carry):
        r0 = pl.multiple_of(g * t, t)
        q = q_ref[pl.ds(r0, t), :]
        out = None
        for hh in range(2):
            head_mask = low if hh == 0 else jnp.logical_not(low)
            q_h = jnp.where(head_mask, q, jnp.zeros_like(q))
            s_tiles, k_rows = [], []
            for w in range(BAND_TILES):
                kt = g - (BAND_TILES - 1) + w
                k0 = pl.multiple_of(jnp.maximum(kt, 0) * t, t)
                k_rows.append(k0)
                s = _dot_nt(q_h, k_ref[pl.ds(k0, t), :]) + t_ref[hh, :, w * t:(w + 1) * t]
                if w < BAND_TILES - 1:
                    s = s + jnp.where(kt >= 0, 0.0, NEG_INF)
                s_tiles.append(s)
            m = functools.reduce(jnp.maximum, s_tiles)
            m = jnp.max(m, axis=1, keepdims=True)
            p_tiles = [jnp.exp(s - m) for s in s_tiles]
            l = jnp.sum(functools.reduce(jnp.add, p_tiles), axis=1, keepdims=True)
            acc = None
            for w in range(BAND_TILES):
                pv = _dot(p_tiles[w].astype(BF16), v_refs[hh][pl.ds(k_rows[w], t), :])
                acc = pv if acc is None else acc + pv
            o_h = acc / l
            out = o_h if out is None else out + o_h
        o_ref[pl.ds(r0, t), :] = out
        return carry

    lax.fori_loop(0, n_tiles, q_tile, 0)


def _band_prompt(qb, kb, vb, table, *, seq_len):
    n = qb.shape[0]
    assert seq_len % ATT_TILE == 0
    blk = pl.BlockSpec((seq_len, LANES), lambda b, j: (b, j))
    return pl.pallas_call(
        functools.partial(_band_kernel, n_tiles=seq_len // ATT_TILE),
        grid=(n // seq_len, H_B // 2),
        in_specs=[blk, blk, blk,
                  pl.BlockSpec((2,) + table.shape[1:], lambda b, j: (j, 0, 0))],
        out_specs=blk,
        out_shape=jax.ShapeDtypeStruct((n, MIX_B), F32),
        scratch_shapes=[pltpu.VMEM((seq_len, LANES), BF16), pltpu.VMEM((seq_len, LANES), BF16)],
        compiler_params=_compiler_params(("arbitrary", "arbitrary")),
        name="band_prompt",
    )(qb, kb, vb, table)


def _sample_kernel(qn_ref, qr_ref, ckvn_ref, krtn_ref, qb_ref, kbn_ref, vbn_ref,
                   cckv_ref, ckr_ref, cbk_ref, cbv_ref, t_ref, wukt_ref, wuv_ref,
                   oa_ref, ob_ref, *, n_tok):
    lane = _lane_iota()
    low = lane < V_DIM
    groups = ROPE_PER_BLOCK

    qn = qn_ref[...]
    qr = qr_ref[...].astype(F32)
    q_lat, q_rope_new = [], []
    q_rope_cache = [[] for _ in range(groups)]
    for h in range(H_A):
        pair, par = divmod(h, 2)
        blk = qn[:, pair * LANES:(pair + 1) * LANES]
        head_mask = low if par == 0 else jnp.logical_not(low)
        q_h = jnp.where(head_mask, blk, jnp.zeros_like(blk))
        q_lat.append(_dot(q_h, wukt_ref[pair * LANES:(pair + 1) * LANES, :]))
        rblk, rpos = divmod(h, ROPE_PER_BLOCK)
        r = qr[:, rblk * LANES:(rblk + 1) * LANES]
        own = jnp.logical_and(lane >= rpos * ROPE_DIM, lane < (rpos + 1) * ROPE_DIM)
        r = jnp.where(own, r, 0.0)
        q_rope_new.append(r)
        for u in range(groups):
            shift = ((u - rpos) % ROPE_PER_BLOCK) * ROPE_DIM
            q_rope_cache[u].append(pltpu.roll(r, shift, 1) if shift else r)
    q_lat = jnp.concatenate(q_lat, axis=0).astype(BF16)
    q_rope_new = jnp.concatenate(q_rope_new, axis=0).astype(BF16)

    ckv_c = cckv_ref[0].astype(BF16)
    kr_c = ckr_ref[0].astype(BF16)
    ckv_n = ckvn_ref[...].astype(BF16)
    s_cache = []
    for u in range(groups):
        q_r = jnp.concatenate(q_rope_cache[u], axis=0).astype(BF16)
        s_cache.append(_dot_nt(q_lat, ckv_c[:, u * KV_RANK:(u + 1) * KV_RANK]) + _dot_nt(q_r, kr_c))
    s_new = _dot_nt(q_lat, ckv_n) + _dot_nt(q_rope_new, krtn_ref[...])
    m = jnp.max(functools.reduce(jnp.maximum, s_cache), axis=1, keepdims=True)
    m = jnp.maximum(m, jnp.max(s_new, axis=1, keepdims=True))
    p_cache = [jnp.exp(s - m) for s in s_cache]
    p_new = jnp.exp(s_new - m)
    l = (jnp.sum(functools.reduce(jnp.add, p_cache), axis=1, keepdims=True)
         + jnp.sum(p_new, axis=1, keepdims=True))
    o_lat = _dot(p_new.astype(BF16), ckv_n)
    for u in range(groups):
        o_lat = o_lat + _dot(p_cache[u].astype(BF16), ckv_c[:, u * KV_RANK:(u + 1) * KV_RANK])
    o_lat = (o_lat / l).astype(BF16)
    wuv = wuv_ref[...]
    col_head = lax.broadcasted_iota(jnp.int32, (1, MIX_A), 1) // V_DIM
    out_a = None
    for h in range(H_A):
        w_h = jnp.where(col_head == h, wuv, jnp.zeros_like(wuv))
        o_h = _dot(o_lat[h * n_tok:(h + 1) * n_tok], w_h)
        out_a = o_h if out_a is None else out_a + o_h
    oa_ref[...] = out_a

    n_cache = cbk_ref.shape[1]
    qb = qb_ref[...]
    kb_n = kbn_ref[...]
    vb_n = vbn_ref[...]
    band_off = BAND_PAST - n_cache
    out_pairs = []
    for pair in range(H_B // 2):
        cols = slice(pair * LANES, (pair + 1) * LANES)
        blk = qb[:, cols]
        q2 = jnp.concatenate([jnp.where(low, blk, jnp.zeros_like(blk)),
                              jnp.where(low, jnp.zeros_like(blk), blk)], axis=0)
        bias = jnp.concatenate([t_ref[2 * pair], t_ref[2 * pair + 1]], axis=0)
        k_c = cbk_ref[0, :, cols].astype(BF16)
        v_c = cbv_ref[0, :, cols].astype(BF16)
        s_c = _dot_nt(q2, k_c) + bias[:, band_off:BAND_PAST]
        s_n = _dot_nt(q2, kb_n[:, cols]) + bias[:, BAND_PAST:BAND_PAST + n_tok]
        m = jnp.maximum(jnp.max(s_c, axis=1, keepdims=True), jnp.max(s_n, axis=1, keepdims=True))
        p_c = jnp.exp(s_c - m)
        p_n = jnp.exp(s_n - m)
        l = jnp.sum(p_c, axis=1, keepdims=True) + jnp.sum(p_n, axis=1, keepdims=True)
        o2 = (_dot(p_c.astype(BF16), v_c) + _dot(p_n.astype(BF16), vb_n[:, cols])) / l
        out_pairs.append(jnp.where(low, o2[0:n_tok], o2[n_tok:2 * n_tok]))
    ob_ref[...] = jnp.concatenate(out_pairs, axis=1)


def _sample_attention(proj, caches, table, wukt, wuv, *, n_streams, n_tok):
    qn, qr, _, krt, _, qb, kb, vb, ckv, _, _, _ = proj
    cache_ckv, cache_kr, cache_bk, cache_bv = caches
    past = cache_ckv.shape[1]
    fold = LANES // ROPE_DIM
    assert past % fold == 0 and cache_bk.shape[1] <= BAND_PAST and BAND_PAST + n_tok <= table.shape[2]
    cckv = cache_ckv.reshape(n_streams, past // fold, fold * KV_RANK)
    ckr = cache_kr.reshape(n_streams, past // fold, fold * ROPE_DIM)
    n_band = cache_bk.shape[1]
    cbk = cache_bk.reshape(n_streams, n_band, MIX_B)
    cbv = cache_bv.reshape(n_streams, n_band, MIX_B)
    tok = lambda width: pl.BlockSpec((n_tok, width), lambda b: (b, 0))
    per_stream = lambda a: pl.BlockSpec((1,) + a.shape[1:], lambda b: (b, 0, 0))
    out = jax.ShapeDtypeStruct((n_streams * n_tok, MIX_A), F32)
    return pl.pallas_call(
        functools.partial(_sample_kernel, n_tok=n_tok),
        grid=(n_streams,),
        in_specs=[
            tok(qn.shape[1]), tok(qr.shape[1]), tok(ckv.shape[1]), tok(krt.shape[1]),
            tok(qb.shape[1]), tok(kb.shape[1]), tok(vb.shape[1]),
            per_stream(cckv), per_stream(ckr), per_stream(cbk), per_stream(cbv),
            pl.BlockSpec((H_B, n_tok, table.shape[2]), lambda b: (0, 0, 0)),
            _const_spec(wukt.shape), _const_spec(wuv.shape),
        ],
        out_specs=(tok(MIX_A), tok(MIX_B)),
        out_shape=(out, out),
        compiler_params=_compiler_params(("arbitrary",)),
        name="sample_attention",
    )(qn, qr, ckv, krt, qb, kb, vb, cckv, ckr, cbk, cbv, table, wukt, wuv)


def _ffn_chunks(d_ff):
    chunks, start = [], 0
    while start < d_ff:
        size = min(2 * MXU_DIM, d_ff - start)
        chunks.append((start, size))
        start += size
    return chunks


def _out_kernel(x_ref, oa_ref, ob_ref, goa_ref, gob_ref, gffn_ref, gfin_ref,
                wout_ref, wg_ref, wu_ref, wd_ref, y_ref):
    mix = jnp.concatenate([_rms(oa_ref[...], goa_ref[...]), _rms(ob_ref[...], gob_ref[...])],
                          axis=1).astype(BF16)
    x1 = x_ref[...] + _dot(mix, wout_ref[...])
    h = _rms(x1, gffn_ref[...]).astype(BF16)
    ffn = None
    for start, size in _ffn_chunks(wg_ref.shape[1]):
        gate = _dot(h, wg_ref[:, start:start + size])
        up = _dot(h, wu_ref[:, start:start + size])
        act = (jax.nn.silu(gate) * up).astype(BF16)
        part = _dot(act, wd_ref[start:start + size, :])
        ffn = part if ffn is None else ffn + part
    y_ref[...] = _rms(x1 + ffn, gfin_ref[...])


def _output(x2d, oa, ob, weights, *, tm):
    n, d = x2d.shape
    assert n % tm == 0
    row = lambda width: pl.BlockSpec((tm, width), lambda i: (i, 0))
    return pl.pallas_call(
        _out_kernel,
        grid=(n // tm,),
        in_specs=[row(d), row(MIX_A), row(MIX_B)] + [_const_spec(w.shape) for w in weights],
        out_specs=row(d),
        out_shape=jax.ShapeDtypeStruct((n, d), F32),
        compiler_params=_compiler_params(("arbitrary",)),
        name="output_ffn",
    )(x2d, oa, ob, *weights)


def _rotate_half_cols(w):
    half = ROPE_DIM // 2
    return jnp.concatenate([-w[..., half:], w[..., :half]], axis=-1)


def _layout_weights(w_in, w_uq, w_uk, w_uv):
    d = w_in.shape[0]
    o = 0
    w_cq = w_in[:, o:o + Q_RANK]; o += Q_RANK
    w_ckv = w_in[:, o:o + KV_RANK]; o += KV_RANK
    w_kr = w_in[:, o:o + ROPE_DIM]; o += ROPE_DIM
    w_qb = w_in[:, o:o + MIX_B]; o += MIX_B
    w_kvb = w_in[:, o:]
    w1 = jnp.concatenate(
        [w_cq, w_ckv, jnp.tile(w_kr, (1, ROPE_PER_BLOCK)),
         jnp.tile(_rotate_half_cols(w_kr), (1, ROPE_PER_BLOCK)),
         w_qb * BAND_SCALE, w_kvb], axis=1).astype(BF16)
    uq = w_uq.reshape(Q_RANK, H_A, NOPE_DIM + ROPE_DIM)
    uq_rope = uq[:, :, NOPE_DIM:]
    wq2 = jnp.concatenate(
        [uq[:, :, :NOPE_DIM].reshape(Q_RANK, H_A * NOPE_DIM),
         uq_rope.reshape(Q_RANK, H_A * ROPE_DIM),
         _rotate_half_cols(uq_rope).reshape(Q_RANK, H_A * ROPE_DIM)], axis=1).astype(BF16)
    uk = w_uk.reshape(KV_RANK, H_A * NOPE_DIM)
    uv = w_uv.reshape(KV_RANK, H_A * V_DIM)
    wkv2 = jnp.concatenate([uk, uv], axis=1).astype(BF16)
    return w1, wq2, wkv2, uk.T.astype(BF16), uv.astype(BF16)


def _rope_table(pos):
    inv = ROPE_THETA ** (-jnp.arange(0, ROPE_DIM, 2, dtype=F32) / ROPE_DIM)
    ang = pos.astype(F32)[:, None] * inv[None, :]
    cos = jnp.tile(jnp.cos(ang), (1, 2 * ROPE_PER_BLOCK))
    sin = jnp.tile(jnp.sin(ang), (1, 2 * ROPE_PER_BLOCK))
    return jnp.concatenate([cos, sin], axis=1)


def _row_tile(n, cap):
    tm = min(n, cap)
    assert n % tm == 0
    return tm


def kernel(x_prompt, x_sample, cache_mla_ckv, cache_mla_krope, cache_band_k, cache_band_v,
           w_in, g_attn, g_q, w_uq, g_kv, w_uk, w_uv, rel_bias, g_out_a, g_out_b, w_out,
           g_ffn, w_gate, w_up, w_down, g_final):
    depth = w_in.shape[0]
    assert depth == 1, "single-layer trunk"
    batch, seq, d = x_prompt.shape
    n_streams, n_tok, _ = x_sample.shape
    past = cache_mla_ckv.shape[2]

    w1, wq2, wkv2, wukt, wuv = _layout_weights(w_in[0], w_uq[0], w_uk[0], w_uv[0])
    proj_w = (g_attn, g_q, g_kv, w1, wq2, wkv2)
    out_w = (g_out_a, g_out_b, g_ffn, g_final[None, :], w_out[0].astype(BF16),
             w_gate[0].astype(BF16), w_up[0].astype(BF16), w_down[0].astype(BF16))
    table = _bias_table(rel_bias[0])

    n_keep = min(BAND_PAST, seq)
    xp = x_prompt.reshape(batch * seq, d)
    tm = _row_tile(n_keep, 512)
    proj = _project(xp, _rope_table(jnp.arange(seq, dtype=jnp.int32)), proj_w,
                    seq_len=seq, n_keep=n_keep, tm=tm)
    qn, qr, kn, krt, v, qb, kb, vb, ckv_p, kr_p, bk_p, bv_p = proj
    oa = _mla_prompt(qn, qr, kn, krt, v, seq_len=seq)
    ob = _band_prompt(qb, kb, vb, table, seq_len=seq)
    y_prompt = _output(xp, oa, ob, out_w, tm=tm).reshape(batch, seq, d)

    n_s = n_streams * n_tok
    xs = x_sample.reshape(n_s, d)
    pos_s = past + jnp.tile(jnp.arange(n_tok, dtype=jnp.int32), n_streams)
    proj_s = _project(xs, _rope_table(pos_s), proj_w, seq_len=n_s, n_keep=n_s, tm=n_s)
    oa_s, ob_s = _sample_attention(
        proj_s, (cache_mla_ckv[0], cache_mla_krope[0], cache_band_k[0], cache_band_v[0]),
        table, wukt, wuv, n_streams=n_streams, n_tok=n_tok)
    y_sample = _output(xs, oa_s, ob_s, out_w, tm=n_s).reshape(n_streams, n_tok, d)
    ckv_s, kr_s, bk_s, bv_s = proj_s[8:12]

    return (
        y_prompt, y_sample,
        ckv_p.reshape(1, batch, seq, KV_RANK), kr_p.reshape(1, batch, seq, ROPE_DIM),
        bk_p.reshape(1, batch, n_keep, H_B, HD_B), bv_p.reshape(1, batch, n_keep, H_B, HD_B),
        ckv_s.reshape(1, n_streams, n_tok, KV_RANK), kr_s.reshape(1, n_streams, n_tok, ROPE_DIM),
        bk_s.reshape(1, n_streams, n_tok, H_B, HD_B), bv_s.reshape(1, n_streams, n_tok, H_B, HD_B),
    )
```

```python
import functools

import jax
import jax.numpy as jnp
from jax import lax
from jax.experimental import pallas as pl
from jax.experimental.pallas import tpu as pltpu

CHUNK = 64
EPS = 1e-6
NEG_INF = -1e30
H_A = 8
NOPE_DIM = 64
ROPE_DIM = 32
V_DIM = 64
Q_RANK = 256
KV_RANK = 256
ROPE_THETA = 10000.0
MLA_SCALE = (NOPE_DIM + ROPE_DIM) ** -0.5
H_B = 8
HD_B = 64
N_PREV_CHUNKS = 8
BAND_PAST = N_PREV_CHUNKS * CHUNK
REL_CLIP = 256
BAND_SCALE = HD_B ** -0.5
LOG2E = 1.4426950408889634
MIX_A = H_A * V_DIM
MIX_B = H_B * HD_B

LANES = 128
MXU_DIM = 256
VMEM_LIMIT_BYTES = 56 * 1024 * 1024

ATT_TILE = 4 * CHUNK
BAND_TILES = N_PREV_CHUNKS * CHUNK // ATT_TILE + 1
PAIR = 2 * V_DIM
ROPE_PER_BLOCK = LANES // ROPE_DIM

F32 = jnp.float32
BF16 = jnp.bfloat16


def _dot(a, b):
    return jnp.dot(a, b, preferred_element_type=F32)


def _dot_nt(a, b):
    return lax.dot_general(a, b, (((1,), (1,)), ((), ())), preferred_element_type=F32)


def _rms(x, g):
    return x * lax.rsqrt(jnp.mean(x * x, axis=-1, keepdims=True) + EPS) * g


def _lane_iota(width=LANES):
    return lax.broadcasted_iota(jnp.int32, (1, width), 1)


def _compiler_params(semantics):
    return pltpu.CompilerParams(dimension_semantics=semantics, vmem_limit_bytes=VMEM_LIMIT_BYTES)


def _const_spec(shape):
    nd = len(shape)
    return pl.BlockSpec(shape, lambda *_: (0,) * nd, pipeline_mode=pl.Buffered(1))


def _bias_kernel(g_ref, t_ref):
    rows, width = t_ref.shape[1], g_ref.shape[2]
    x = jnp.broadcast_to(g_ref[0], (rows, width))
    x = pltpu.roll(x, 0, 1, stride=1, stride_axis=0)
    x = x[:, width - t_ref.shape[2]:]
    qc = lax.broadcasted_iota(jnp.int32, x.shape, 0) // CHUNK
    kc = lax.broadcasted_iota(jnp.int32, x.shape, 1) // CHUNK
    visible = jnp.logical_and(kc >= qc, kc <= qc + N_PREV_CHUNKS)
    t_ref[0] = jnp.where(visible, x * LOG2E, NEG_INF)


def _bias_table(rel_bias):
    n_keys = BAND_TILES * ATT_TILE
    width = n_keys + ATT_TILE
    n_const = width - 2 * REL_CLIP + 1
    g = jnp.concatenate(
        [jnp.broadcast_to(rel_bias[:, 2 * REL_CLIP:], (H_B, n_const)),
         rel_bias[:, 2 * REL_CLIP - 1:0:-1]], axis=1)
    g = g.reshape(H_B, 1, width)
    return pl.pallas_call(
        _bias_kernel,
        grid=(H_B,),
        in_specs=[pl.BlockSpec((1, 1, width), lambda h: (h, 0, 0))],
        out_specs=pl.BlockSpec((1, ATT_TILE, n_keys), lambda h: (h, 0, 0)),
        out_shape=jax.ShapeDtypeStruct((H_B, ATT_TILE, n_keys), F32),
        compiler_params=_compiler_params(("arbitrary",)),
        name="bias_table",
    )(g)


def _proj_kernel(x_ref, tab_ref, ga_ref, gq_ref, gkv_ref, w1_ref, wq2_ref, wkv2_ref,
                 qn_ref, qr_ref, kn_ref, krt_ref, v_ref, qb_ref, kb_ref, vb_ref,
                 ckv_ref, kr_ref, bk_ref, bv_ref, *, tiles_per_seq, keep_from_tile):
    xn = _rms(x_ref[...], ga_ref[...]).astype(BF16)
    cos = tab_ref[:, 0:LANES]
    sin = tab_ref[:, LANES:2 * LANES]

    c_q = _dot(xn, w1_ref[:, 0:Q_RANK])
    cqn = _rms(c_q, gq_ref[...]).astype(BF16)
    q2 = _dot(cqn, wq2_ref[...])
    n_nope = H_A * NOPE_DIM
    n_rope = H_A * ROPE_DIM
    qn_ref[...] = (q2[:, 0:n_nope] * (MLA_SCALE * LOG2E)).astype(BF16)
    cos2 = jnp.concatenate([cos] * (n_rope // LANES), axis=1)
    sin2 = jnp.concatenate([sin] * (n_rope // LANES), axis=1)
    q_rope = q2[:, n_nope:n_nope + n_rope] * cos2 + q2[:, n_nope + n_rope:] * sin2
    qr_ref[...] = (q_rope * (MLA_SCALE * LOG2E)).astype(BF16)

    o = Q_RANK
    c_kv = _dot(xn, w1_ref[:, o:o + KV_RANK])
    ckvn = _rms(c_kv, gkv_ref[...])
    ckv_ref[...] = ckvn
    kv2 = _dot(ckvn.astype(BF16), wkv2_ref[...])
    kn_ref[...] = kv2[:, 0:n_nope].astype(BF16)
    v_ref[...] = kv2[:, n_nope:].astype(BF16)

    o += KV_RANK
    krr = _dot(xn, w1_ref[:, o:o + 2 * LANES])
    k_rope = krr[:, 0:LANES] * cos + krr[:, LANES:] * sin
    krt_ref[...] = k_rope.astype(BF16)
    kr_ref[...] = k_rope[:, 0:ROPE_DIM]

    o += 2 * LANES
    qkv = _dot(xn, w1_ref[:, o:o + 3 * MIX_B])
    qb_ref[...] = (qkv[:, 0:MIX_B] * (BAND_SCALE * LOG2E)).astype(BF16)
    kb = qkv[:, MIX_B:2 * MIX_B]
    vb = qkv[:, 2 * MIX_B:]
    kb_ref[...] = kb.astype(BF16)
    vb_ref[...] = vb.astype(BF16)

    @pl.when(pl.program_id(0) % tiles_per_seq >= keep_from_tile)
    def _():
        bk_ref[0] = kb
        bv_ref[0] = vb


def _project(x2d, tab, weights, *, seq_len, n_keep, tm):
    n, d = x2d.shape
    g_attn, g_q, g_kv, w1, wq2, wkv2 = weights
    assert n % seq_len == 0 and seq_len % tm == 0 and n_keep % tm == 0
    tiles_per_seq = seq_len // tm
    keep_from_tile = (seq_len - n_keep) // tm
    n_seq = n // seq_len
    keep_tiles = n_keep // tm

    row = lambda width: pl.BlockSpec((tm, width), lambda i: (i, 0))
    keep_spec = pl.BlockSpec(
        (1, tm, MIX_B),
        lambda i: (i // tiles_per_seq, jnp.maximum(i % tiles_per_seq - keep_from_tile, 0), 0))
    bf = lambda width: jax.ShapeDtypeStruct((n, width), BF16)
    out_shape = (
        bf(H_A * NOPE_DIM), bf(H_A * ROPE_DIM), bf(H_A * NOPE_DIM), bf(LANES), bf(MIX_A),
        bf(MIX_B), bf(MIX_B), bf(MIX_B),
        jax.ShapeDtypeStruct((n, KV_RANK), F32), jax.ShapeDtypeStruct((n, ROPE_DIM), F32),
        jax.ShapeDtypeStruct((n_seq, keep_tiles * tm, MIX_B), F32),
        jax.ShapeDtypeStruct((n_seq, keep_tiles * tm, MIX_B), F32),
    )
    out_specs = (
        row(H_A * NOPE_DIM), row(H_A * ROPE_DIM), row(H_A * NOPE_DIM), row(LANES), row(MIX_A),
        row(MIX_B), row(MIX_B), row(MIX_B), row(KV_RANK), row(ROPE_DIM), keep_spec, keep_spec,
    )
    in_specs = [
        row(d),
        pl.BlockSpec((tm, 2 * LANES), lambda i: (i % tiles_per_seq, 0)),
        _const_spec(g_attn.shape), _const_spec(g_q.shape), _const_spec(g_kv.shape),
        _const_spec(w1.shape), _const_spec(wq2.shape), _const_spec(wkv2.shape),
    ]
    return pl.pallas_call(
        functools.partial(_proj_kernel, tiles_per_seq=tiles_per_seq, keep_from_tile=keep_from_tile),
        grid=(n // tm,),
        in_specs=in_specs,
        out_specs=out_specs,
        out_shape=out_shape,
        compiler_params=_compiler_params(("arbitrary",)),
        name="projection",
    )(x2d, tab, g_attn, g_q, g_kv, w1, wq2, wkv2)


def _mla_kernel(qn_ref, qr_ref, kn_ref, krt_ref, v_ref, o_ref, ve_ref, vo_ref, *, n_tiles):
    t = ATT_TILE
    lane = _lane_iota()
    low = lane < V_DIM
    v = v_ref[...]
    zero = jnp.zeros_like(v)
    ve_ref[...] = jnp.where(low, v, zero)
    vo_ref[...] = jnp.where(low, zero, v)
    v_refs = (ve_ref, vo_ref)
    rope_base = (pl.program_id(1) % (ROPE_PER_BLOCK // 2)) * 2 * ROPE_DIM
    row_chunk = lax.broadcasted_iota(jnp.int32, (t, t), 0) // CHUNK
    col_chunk = lax.broadcasted_iota(jnp.int32, (t, t), 1) // CHUNK
    causal = col_chunk <= row_chunk

    for qi in range(n_tiles):
        r0, n_keys = qi * t, (qi + 1) * t
        qn = qn_ref[r0:r0 + t, :]
        qr = qr_ref[r0:r0 + t, :]
        k2 = jnp.concatenate([kn_ref[0:n_keys, :], krt_ref[0:n_keys, :]], axis=1)
        out = None
        for hh in range(2):
            nope_mask = low if hh == 0 else jnp.logical_not(low)
            lo = rope_base + hh * ROPE_DIM
            rope_mask = jnp.logical_and(lane >= lo, lane < lo + ROPE_DIM)
            q_h = jnp.concatenate(
                [jnp.where(nope_mask, qn, jnp.zeros_like(qn)),
                 jnp.where(rope_mask, qr, jnp.zeros_like(qr))], axis=1)
            s = _dot_nt(q_h, k2)
            s_diag = jnp.where(causal, s[:, r0:], NEG_INF)
            s = s_diag if qi == 0 else jnp.concatenate([s[:, :r0], s_diag], axis=1)
            p = jnp.exp2(s - jnp.max(s, axis=1, keepdims=True))
            l = jnp.sum(p, axis=1, keepdims=True)
            o_h = _dot(p.astype(BF16), v_refs[hh][0:n_keys, :]) / l
            out = o_h if out is None else out + o_h
        o_ref[r0:r0 + t, :] = out


def _mla_prompt(qn, qr, kn, krt, v, *, seq_len):
    n = qn.shape[0]
    assert seq_len % ATT_TILE == 0
    n_pairs = H_A // 2
    blk = lambda f: pl.BlockSpec((seq_len, LANES), f)
    return pl.pallas_call(
        functools.partial(_mla_kernel, n_tiles=seq_len // ATT_TILE),
        grid=(n // seq_len, n_pairs),
        in_specs=[
            blk(lambda b, j: (b, j)),
            blk(lambda b, j: (b, j // (ROPE_PER_BLOCK // 2))),
            blk(lambda b, j: (b, j)),
            blk(lambda b, j: (b, 0)),
            blk(lambda b, j: (b, j)),
        ],
        out_specs=blk(lambda b, j: (b, j)),
        out_shape=jax.ShapeDtypeStruct((n, MIX_A), F32),
        scratch_shapes=[pltpu.VMEM((seq_len, LANES), BF16), pltpu.VMEM((seq_len, LANES), BF16)],
        compiler_params=_compiler_params(("arbitrary", "arbitrary")),
        name="mla_prompt",
    )(qn, qr, kn, krt, v)


def _band_kernel(q_ref, k_ref, v_ref, t_ref, o_ref, ve_ref, vo_ref, *, n_tiles):
    t = ATT_TILE
    lane = _lane_iota()
    low = lane < HD_B
    v = v_ref[...]
    zero = jnp.zeros_like(v)
    ve_ref[...] = jnp.where(low, v, zero)
    vo_ref[...] = jnp.where(low, zero, v)
    v_refs = (ve_ref, vo_ref)

    for g in range(n_tiles):
        r0 = g * t
        k_lo = max(g - (BAND_TILES - 1), 0) * t
        n_keys = r0 + t - k_lo
        q = q_ref[r0:r0 + t, :]
        k = k_ref[k_lo:r0 + t, :]
        out = None
        for hh in range(2):
            head_mask = low if hh == 0 else jnp.logical_not(low)
            q_h = jnp.where(head_mask, q, jnp.zeros_like(q))
            s = _dot_nt(q_h, k) + t_ref[hh, :, BAND_TILES * t - n_keys:]
            p = jnp.exp2(s - jnp.max(s, axis=1, keepdims=True))
            l = jnp.sum(p, axis=1, keepdims=True)
            o_h = _dot(p.astype(BF16), v_refs[hh][k_lo:r0 + t, :]) / l
            out = o_h if out is None else out + o_h
        o_ref[r0:r0 + t, :] = out


def _band_prompt(qb, kb, vb, table, *, seq_len):
    n = qb.shape[0]
    assert seq_len % ATT_TILE == 0
    blk = pl.BlockSpec((seq_len, LANES), lambda b, j: (b, j))
    return pl.pallas_call(
        functools.partial(_band_kernel, n_tiles=seq_len // ATT_TILE),
        grid=(n // seq_len, H_B // 2),
        in_specs=[blk, blk, blk,
                  pl.BlockSpec((2,) + table.shape[1:], lambda b, j: (j, 0, 0))],
        out_specs=blk,
        out_shape=jax.ShapeDtypeStruct((n, MIX_B), F32),
        scratch_shapes=[pltpu.VMEM((seq_len, LANES), BF16), pltpu.VMEM((seq_len, LANES), BF16)],
        compiler_params=_compiler_params(("arbitrary", "arbitrary")),
        name="band_prompt",
    )(qb, kb, vb, table)


def _sample_kernel(qn_ref, qr_ref, ckvn_ref, krtn_ref, qb_ref, kbn_ref, vbn_ref,
                   cckv_ref, ckr_ref, cbk_ref, cbv_ref, t_ref, wukt_ref, wuv_ref,
                   oa_ref, ob_ref, *, n_tok):
    lane = _lane_iota()
    low = lane < V_DIM
    groups = ROPE_PER_BLOCK

    qn = qn_ref[...]
    qr = qr_ref[...].astype(F32)
    q_lat, q_rope_new = [], []
    q_rope_cache = [[] for _ in range(groups)]
    for h in range(H_A):
        pair, par = divmod(h, 2)
        blk = qn[:, pair * LANES:(pair + 1) * LANES]
        head_mask = low if par == 0 else jnp.logical_not(low)
        q_h = jnp.where(head_mask, blk, jnp.zeros_like(blk))
        q_lat.append(_dot(q_h, wukt_ref[pair * LANES:(pair + 1) * LANES, :]))
        rblk, rpos = divmod(h, ROPE_PER_BLOCK)
        r = qr[:, rblk * LANES:(rblk + 1) * LANES]
        own = jnp.logical_and(lane >= rpos * ROPE_DIM, lane < (rpos + 1) * ROPE_DIM)
        r = jnp.where(own, r, 0.0)
        q_rope_new.append(r)
        for u in range(groups):
            shift = ((u - rpos) % ROPE_PER_BLOCK) * ROPE_DIM
            q_rope_cache[u].append(pltpu.roll(r, shift, 1) if shift else r)
    q_lat = jnp.concatenate(q_lat, axis=0).astype(BF16)
    q_rope_new = jnp.concatenate(q_rope_new, axis=0).astype(BF16)

    ckv_c = cckv_ref[0].astype(BF16)
    kr_c = ckr_ref[0].astype(BF16)
    ckv_n = ckvn_ref[...].astype(BF16)
    s_cache = []
    for u in range(groups):
        q_r = jnp.concatenate(q_rope_cache[u], axis=0).astype(BF16)
        s_cache.append(_dot_nt(q_lat, ckv_c[:, u * KV_RANK:(u + 1) * KV_RANK]) + _dot_nt(q_r, kr_c))
    s_new = _dot_nt(q_lat, ckv_n) + _dot_nt(q_rope_new, krtn_ref[...])
    m = jnp.max(functools.reduce(jnp.maximum, s_cache), axis=1, keepdims=True)
    m = jnp.maximum(m, jnp.max(s_new, axis=1, keepdims=True))
    p_cache = [jnp.exp2(s - m) for s in s_cache]
    p_new = jnp.exp2(s_new - m)
    l = (jnp.sum(functools.reduce(jnp.add, p_cache), axis=1, keepdims=True)
         + jnp.sum(p_new, axis=1, keepdims=True))
    o_lat = _dot(p_new.astype(BF16), ckv_n)
    for u in range(groups):
        o_lat = o_lat + _dot(p_cache[u].astype(BF16), ckv_c[:, u * KV_RANK:(u + 1) * KV_RANK])
    o_lat = (o_lat / l).astype(BF16)
    wuv = wuv_ref[...]
    col_head = lax.broadcasted_iota(jnp.int32, (1, MIX_A), 1) // V_DIM
    out_a = None
    for h in range(H_A):
        w_h = jnp.where(col_head == h, wuv, jnp.zeros_like(wuv))
        o_h = _dot(o_lat[h * n_tok:(h + 1) * n_tok], w_h)
        out_a = o_h if out_a is None else out_a + o_h
    oa_ref[...] = out_a

    n_cache = cbk_ref.shape[1]
    qb = qb_ref[...]
    kb_n = kbn_ref[...]
    vb_n = vbn_ref[...]
    band_off = BAND_PAST - n_cache
    out_pairs = []
    for pair in range(H_B // 2):
        cols = slice(pair * LANES, (pair + 1) * LANES)
        blk = qb[:, cols]
        q2 = jnp.concatenate([jnp.where(low, blk, jnp.zeros_like(blk)),
                              jnp.where(low, jnp.zeros_like(blk), blk)], axis=0)
        bias = jnp.concatenate([t_ref[2 * pair], t_ref[2 * pair + 1]], axis=0)
        k_c = cbk_ref[0, :, cols].astype(BF16)
        v_c = cbv_ref[0, :, cols].astype(BF16)
        s_c = _dot_nt(q2, k_c) + bias[:, band_off:BAND_PAST]
        s_n = _dot_nt(q2, kb_n[:, cols]) + bias[:, BAND_PAST:BAND_PAST + n_tok]
        m = jnp.maximum(jnp.max(s_c, axis=1, keepdims=True), jnp.max(s_n, axis=1, keepdims=True))
        p_c = jnp.exp2(s_c - m)
        p_n = jnp.exp2(s_n - m)
        l = jnp.sum(p_c, axis=1, keepdims=True) + jnp.sum(p_n, axis=1, keepdims=True)
        o2 = (_dot(p_c.astype(BF16), v_c) + _dot(p_n.astype(BF16), vb_n[:, cols])) / l
        out_pairs.append(jnp.where(low, o2[0:n_tok], o2[n_tok:2 * n_tok]))
    ob_ref[...] = jnp.concatenate(out_pairs, axis=1)


def _sample_attention(proj, caches, table, wukt, wuv, *, n_streams, n_tok):
    qn, qr, _, krt, _, qb, kb, vb, ckv, _, _, _ = proj
    cache_ckv, cache_kr, cache_bk, cache_bv = caches
    past = cache_ckv.shape[1]
    fold = LANES // ROPE_DIM
    assert past % fold == 0 and cache_bk.shape[1] <= BAND_PAST and BAND_PAST + n_tok <= table.shape[2]
    cckv = cache_ckv.reshape(n_streams, past // fold, fold * KV_RANK)
    ckr = cache_kr.reshape(n_streams, past // fold, fold * ROPE_DIM)
    n_band = cache_bk.shape[1]
    cbk = cache_bk.reshape(n_streams, n_band, MIX_B)
    cbv = cache_bv.reshape(n_streams, n_band, MIX_B)
    tok = lambda width: pl.BlockSpec((n_tok, width), lambda b: (b, 0))
    per_stream = lambda a: pl.BlockSpec((1,) + a.shape[1:], lambda b: (b, 0, 0))
    out = jax.ShapeDtypeStruct((n_streams * n_tok, MIX_A), F32)
    return pl.pallas_call(
        functools.partial(_sample_kernel, n_tok=n_tok),
        grid=(n_streams,),
        in_specs=[
            tok(qn.shape[1]), tok(qr.shape[1]), tok(ckv.shape[1]), tok(krt.shape[1]),
            tok(qb.shape[1]), tok(kb.shape[1]), tok(vb.shape[1]),
            per_stream(cckv), per_stream(ckr), per_stream(cbk), per_stream(cbv),
            pl.BlockSpec((H_B, n_tok, table.shape[2]), lambda b: (0, 0, 0)),
            _const_spec(wukt.shape), _const_spec(wuv.shape),
        ],
        out_specs=(tok(MIX_A), tok(MIX_B)),
        out_shape=(out, out),
        compiler_params=_compiler_params(("arbitrary",)),
        name="sample_attention",
    )(qn, qr, ckv, krt, qb, kb, vb, cckv, ckr, cbk, cbv, table, wukt, wuv)


def _ffn_chunks(d_ff):
    chunks, start = [], 0
    while start < d_ff:
        size = min(2 * MXU_DIM, d_ff - start)
        chunks.append((start, size))
        start += size
    return chunks


def _out_kernel(x_ref, oa_ref, ob_ref, goa_ref, gob_ref, gffn_ref, gfin_ref,
                wout_ref, wg_ref, wu_ref, wd_ref, y_ref):
    mix = jnp.concatenate([_rms(oa_ref[...], goa_ref[...]), _rms(ob_ref[...], gob_ref[...])],
                          axis=1).astype(BF16)
    x1 = x_ref[...] + _dot(mix, wout_ref[...])
    h = _rms(x1, gffn_ref[...]).astype(BF16)
    ffn = None
    for start, size in _ffn_chunks(wg_ref.shape[1]):
        gate = _dot(h, wg_ref[:, start:start + size])
        up = _dot(h, wu_ref[:, start:start + size])
        act = (jax.nn.silu(gate) * up).astype(BF16)
        part = _dot(act, wd_ref[start:start + size, :])
        ffn = part if ffn is None else ffn + part
    y_ref[...] = _rms(x1 + ffn, gfin_ref[...])


def _output(x2d, oa, ob, weights, *, tm):
    n, d = x2d.shape
    assert n % tm == 0
    row = lambda width: pl.BlockSpec((tm, width), lambda i: (i, 0))
    return pl.pallas_call(
        _out_kernel,
        grid=(n // tm,),
        in_specs=[row(d), row(MIX_A), row(MIX_B)] + [_const_spec(w.shape) for w in weights],
        out_specs=row(d),
        out_shape=jax.ShapeDtypeStruct((n, d), F32),
        compiler_params=_compiler_params(("arbitrary",)),
        name="output_ffn",
    )(x2d, oa, ob, *weights)


def _rotate_half_cols(w):
    half = ROPE_DIM // 2
    return jnp.concatenate([-w[..., half:], w[..., :half]], axis=-1)


def _layout_weights(w_in, w_uq, w_uk, w_uv):
    d = w_in.shape[0]
    o = 0
    w_cq = w_in[:, o:o + Q_RANK]; o += Q_RANK
    w_ckv = w_in[:, o:o + KV_RANK]; o += KV_RANK
    w_kr = w_in[:, o:o + ROPE_DIM]; o += ROPE_DIM
    w_qb = w_in[:, o:o + MIX_B]; o += MIX_B
    w_kvb = w_in[:, o:]
    w1 = jnp.concatenate(
        [w_cq, w_ckv, jnp.tile(w_kr, (1, ROPE_PER_BLOCK)),
         jnp.tile(_rotate_half_cols(w_kr), (1, ROPE_PER_BLOCK)),
         w_qb, w_kvb], axis=1).astype(BF16)
    uq = w_uq.reshape(Q_RANK, H_A, NOPE_DIM + ROPE_DIM)
    uq_rope = uq[:, :, NOPE_DIM:]
    wq2 = jnp.concatenate(
        [uq[:, :, :NOPE_DIM].reshape(Q_RANK, H_A * NOPE_DIM),
         uq_rope.reshape(Q_RANK, H_A * ROPE_DIM),
         _rotate_half_cols(uq_rope).reshape(Q_RANK, H_A * ROPE_DIM)], axis=1).astype(BF16)
    uk = w_uk.reshape(KV_RANK, H_A * NOPE_DIM)
    uv = w_uv.reshape(KV_RANK, H_A * V_DIM)
    wkv2 = jnp.concatenate([uk, uv], axis=1).astype(BF16)
    return w1, wq2, wkv2, uk.T.astype(BF16), uv.astype(BF16)


def _rope_table(pos):
    inv = ROPE_THETA ** (-jnp.arange(0, ROPE_DIM, 2, dtype=F32) / ROPE_DIM)
    ang = pos.astype(F32)[:, None] * inv[None, :]
    cos = jnp.tile(jnp.cos(ang), (1, 2 * ROPE_PER_BLOCK))
    sin = jnp.tile(jnp.sin(ang), (1, 2 * ROPE_PER_BLOCK))
    return jnp.concatenate([cos, sin], axis=1)


def _row_tile(n, cap):
    tm = min(n, cap)
    assert n % tm == 0
    return tm


def kernel(x_prompt, x_sample, cache_mla_ckv, cache_mla_krope, cache_band_k, cache_band_v,
           w_in, g_attn, g_q, w_uq, g_kv, w_uk, w_uv, rel_bias, g_out_a, g_out_b, w_out,
           g_ffn, w_gate, w_up, w_down, g_final):
    depth = w_in.shape[0]
    assert depth == 1, "single-layer trunk"
    batch, seq, d = x_prompt.shape
    n_streams, n_tok, _ = x_sample.shape
    past = cache_mla_ckv.shape[2]

    w1, wq2, wkv2, wukt, wuv = _layout_weights(w_in[0], w_uq[0], w_uk[0], w_uv[0])
    proj_w = (g_attn, g_q, g_kv, w1, wq2, wkv2)
    out_w = (g_out_a, g_out_b, g_ffn, g_final[None, :], w_out[0].astype(BF16),
             w_gate[0].astype(BF16), w_up[0].astype(BF16), w_down[0].astype(BF16))
    table = _bias_table(rel_bias[0])

    n_keep = min(BAND_PAST, seq)
    xp = x_prompt.reshape(batch * seq, d)
    tm = _row_tile(n_keep, 512)
    proj = _project(xp, _rope_table(jnp.arange(seq, dtype=jnp.int32)), proj_w,
                    seq_len=seq, n_keep=n_keep, tm=tm)
    qn, qr, kn, krt, v, qb, kb, vb, ckv_p, kr_p, bk_p, bv_p = proj
    oa = _mla_prompt(qn, qr, kn, krt, v, seq_len=seq)
    ob = _band_prompt(qb, kb, vb, table, seq_len=seq)
    y_prompt = _output(xp, oa, ob, out_w, tm=tm).reshape(batch, seq, d)

    n_s = n_streams * n_tok
    xs = x_sample.reshape(n_s, d)
    pos_s = past + jnp.tile(jnp.arange(n_tok, dtype=jnp.int32), n_streams)
    proj_s = _project(xs, _rope_table(pos_s), proj_w, seq_len=n_s, n_keep=n_s, tm=n_s)
    oa_s, ob_s = _sample_attention(
        proj_s, (cache_mla_ckv[0], cache_mla_krope[0], cache_band_k[0], cache_band_v[0]),
        table, wukt, wuv, n_streams=n_streams, n_tok=n_tok)
    y_sample = _output(xs, oa_s, ob_s, out_w, tm=n_s).reshape(n_streams, n_tok, d)
    ckv_s, kr_s, bk_s, bv_s = proj_s[8:12]

    return (
        y_prompt, y_sample,
        ckv_p.reshape(1, batch, seq, KV_RANK), kr_p.reshape(1, batch, seq, ROPE_DIM),
        bk_p.reshape(1, batch, n_keep, H_B, HD_B), bv_p.reshape(1, batch, n_keep, H_B, HD_B),
        ckv_s.reshape(1, n_streams, n_tok, KV_RANK), kr_s.reshape(1, n_streams, n_tok, ROPE_DIM),
        bk_s.reshape(1, n_streams, n_tok, H_B, HD_B), bv_s.reshape(1, n_streams, n_tok, H_B, HD_B),
    )
```

```python
import functools

import jax
import jax.numpy as jnp
from jax import lax
from jax.experimental import pallas as pl
from jax.experimental.pallas import tpu as pltpu

CHUNK = 64
EPS = 1e-6
NEG_INF = -1e30
H_A = 8
NOPE_DIM = 64
ROPE_DIM = 32
V_DIM = 64
Q_RANK = 256
KV_RANK = 256
ROPE_THETA = 10000.0
MLA_SCALE = (NOPE_DIM + ROPE_DIM) ** -0.5
H_B = 8
HD_B = 64
N_PREV_CHUNKS = 8
BAND_PAST = N_PREV_CHUNKS * CHUNK
REL_CLIP = 256
BAND_SCALE = HD_B ** -0.5
LOG2E = 1.4426950408889634
MIX_A = H_A * V_DIM
MIX_B = H_B * HD_B

LANES = 128
MXU_DIM = 256
VMEM_LIMIT_BYTES = 56 * 1024 * 1024

ATT_TILE = 4 * CHUNK
BAND_TILES = N_PREV_CHUNKS * CHUNK // ATT_TILE + 1
PAIR = 2 * V_DIM
ROPE_PER_BLOCK = LANES // ROPE_DIM

F32 = jnp.float32
BF16 = jnp.bfloat16


def _dot(a, b):
    return jnp.dot(a, b, preferred_element_type=F32)


def _dot_nt(a, b):
    return lax.dot_general(a, b, (((1,), (1,)), ((), ())), preferred_element_type=F32)


def _rms(x, g):
    return x * lax.rsqrt(jnp.mean(x * x, axis=-1, keepdims=True) + EPS) * g


def _lane_iota(width=LANES):
    return lax.broadcasted_iota(jnp.int32, (1, width), 1)


def _compiler_params(semantics):
    return pltpu.CompilerParams(dimension_semantics=semantics, vmem_limit_bytes=VMEM_LIMIT_BYTES)


def _const_spec(shape):
    nd = len(shape)
    return pl.BlockSpec(shape, lambda *_: (0,) * nd, pipeline_mode=pl.Buffered(1))


def _bias_kernel(g_ref, t_ref):
    rows, width = t_ref.shape[1], g_ref.shape[2]
    x = jnp.broadcast_to(g_ref[0], (rows, width))
    x = pltpu.roll(x, 0, 1, stride=1, stride_axis=0)
    x = x[:, width - t_ref.shape[2]:]
    qc = lax.broadcasted_iota(jnp.int32, x.shape, 0) // CHUNK
    kc = lax.broadcasted_iota(jnp.int32, x.shape, 1) // CHUNK
    visible = jnp.logical_and(kc >= qc, kc <= qc + N_PREV_CHUNKS)
    t_ref[0] = jnp.where(visible, x * LOG2E, NEG_INF)


def _bias_table(rel_bias):
    n_keys = BAND_TILES * ATT_TILE
    width = n_keys + ATT_TILE
    n_const = width - 2 * REL_CLIP + 1
    g = jnp.concatenate(
        [jnp.broadcast_to(rel_bias[:, 2 * REL_CLIP:], (H_B, n_const)),
         rel_bias[:, 2 * REL_CLIP - 1:0:-1]], axis=1)
    g = g.reshape(H_B, 1, width)
    return pl.pallas_call(
        _bias_kernel,
        grid=(H_B,),
        in_specs=[pl.BlockSpec((1, 1, width), lambda h: (h, 0, 0))],
        out_specs=pl.BlockSpec((1, ATT_TILE, n_keys), lambda h: (h, 0, 0)),
        out_shape=jax.ShapeDtypeStruct((H_B, ATT_TILE, n_keys), F32),
        compiler_params=_compiler_params(("arbitrary",)),
        name="bias_table",
    )(g)


def _proj_kernel(x_ref, tab_ref, ga_ref, gq_ref, gkv_ref, w1_ref, wq2_ref, wkv2_ref,
                 qn_ref, qr_ref, kn_ref, krt_ref, v_ref, qb_ref, kb_ref, vb_ref,
                 ckv_ref, kr_ref, bk_ref, bv_ref, *, tiles_per_seq, keep_from_tile, seq_minor):
    xn = _rms(x_ref[...], ga_ref[...]).astype(BF16)
    cos = tab_ref[:, 0:LANES]
    sin = tab_ref[:, LANES:2 * LANES]

    c_q = _dot(xn, w1_ref[:, 0:Q_RANK])
    cqn = _rms(c_q, gq_ref[...]).astype(BF16)
    q2 = _dot(cqn, wq2_ref[...])
    n_nope = H_A * NOPE_DIM
    n_rope = H_A * ROPE_DIM
    qn_ref[...] = (q2[:, 0:n_nope] * (MLA_SCALE * LOG2E)).astype(BF16)
    cos2 = jnp.concatenate([cos] * (n_rope // LANES), axis=1)
    sin2 = jnp.concatenate([sin] * (n_rope // LANES), axis=1)
    q_rope = q2[:, n_nope:n_nope + n_rope] * cos2 + q2[:, n_nope + n_rope:] * sin2
    qr_ref[...] = (q_rope * (MLA_SCALE * LOG2E)).astype(BF16)

    o = Q_RANK
    c_kv = _dot(xn, w1_ref[:, o:o + KV_RANK])
    ckvn = _rms(c_kv, gkv_ref[...])
    ckv_ref[...] = ckvn
    kv2 = _dot(ckvn.astype(BF16), wkv2_ref[...])
    kn_ref[...] = kv2[:, 0:n_nope].astype(BF16)
    v_ref[...] = kv2[:, n_nope:].astype(BF16)

    o += KV_RANK
    krr = _dot(xn, w1_ref[:, o:o + 2 * LANES])
    k_rope = krr[:, 0:LANES] * cos + krr[:, LANES:] * sin
    krt_ref[...] = k_rope.astype(BF16)
    if seq_minor:
        kr_ref[0] = k_rope.T[0:ROPE_DIM, :]
    else:
        kr_ref[...] = k_rope[:, 0:ROPE_DIM]

    o += 2 * LANES
    qkv = _dot(xn, w1_ref[:, o:o + 3 * MIX_B])
    qb_ref[...] = (qkv[:, 0:MIX_B] * (BAND_SCALE * LOG2E)).astype(BF16)
    kb = qkv[:, MIX_B:2 * MIX_B]
    vb = qkv[:, 2 * MIX_B:]
    kb_ref[...] = kb.astype(BF16)
    vb_ref[...] = vb.astype(BF16)

    @pl.when(pl.program_id(0) % tiles_per_seq >= keep_from_tile)
    def _():
        bk_ref[0] = kb.T if seq_minor else kb
        bv_ref[0] = vb.T if seq_minor else vb


def _project(x2d, tab, weights, *, seq_len, n_keep, tm, seq_minor):
    n, d = x2d.shape
    g_attn, g_q, g_kv, w1, wq2, wkv2 = weights
    assert n % seq_len == 0 and seq_len % tm == 0 and n_keep % tm == 0
    tiles_per_seq = seq_len // tm
    keep_from_tile = (seq_len - n_keep) // tm
    n_seq = n // seq_len
    keep_tiles = n_keep // tm

    row = lambda width: pl.BlockSpec((tm, width), lambda i: (i, 0))
    keep_tile = lambda i: jnp.maximum(i % tiles_per_seq - keep_from_tile, 0)
    if seq_minor:
        kr_spec = pl.BlockSpec((1, ROPE_DIM, tm), lambda i: (i // tiles_per_seq, 0, i % tiles_per_seq))
        kr_shape = jax.ShapeDtypeStruct((n_seq, ROPE_DIM, seq_len), F32)
        keep_spec = pl.BlockSpec((1, MIX_B, tm), lambda i: (i // tiles_per_seq, 0, keep_tile(i)))
        keep_shape = jax.ShapeDtypeStruct((n_seq, MIX_B, keep_tiles * tm), F32)
    else:
        kr_spec = row(ROPE_DIM)
        kr_shape = jax.ShapeDtypeStruct((n, ROPE_DIM), F32)
        keep_spec = pl.BlockSpec((1, tm, MIX_B), lambda i: (i // tiles_per_seq, keep_tile(i), 0))
        keep_shape = jax.ShapeDtypeStruct((n_seq, keep_tiles * tm, MIX_B), F32)
    bf = lambda width: jax.ShapeDtypeStruct((n, width), BF16)
    out_shape = (
        bf(H_A * NOPE_DIM), bf(H_A * ROPE_DIM), bf(H_A * NOPE_DIM), bf(LANES), bf(MIX_A),
        bf(MIX_B), bf(MIX_B), bf(MIX_B),
        jax.ShapeDtypeStruct((n, KV_RANK), F32), kr_shape, keep_shape, keep_shape,
    )
    out_specs = (
        row(H_A * NOPE_DIM), row(H_A * ROPE_DIM), row(H_A * NOPE_DIM), row(LANES), row(MIX_A),
        row(MIX_B), row(MIX_B), row(MIX_B), row(KV_RANK), kr_spec, keep_spec, keep_spec,
    )
    in_specs = [
        row(d),
        pl.BlockSpec((tm, 2 * LANES), lambda i: (i % tiles_per_seq, 0)),
        _const_spec(g_attn.shape), _const_spec(g_q.shape), _const_spec(g_kv.shape),
        _const_spec(w1.shape), _const_spec(wq2.shape), _const_spec(wkv2.shape),
    ]
    return pl.pallas_call(
        functools.partial(_proj_kernel, tiles_per_seq=tiles_per_seq, keep_from_tile=keep_from_tile,
                          seq_minor=seq_minor),
        grid=(n // tm,),
        in_specs=in_specs,
        out_specs=out_specs,
        out_shape=out_shape,
        compiler_params=_compiler_params(("arbitrary",)),
        name="projection",
    )(x2d, tab, g_attn, g_q, g_kv, w1, wq2, wkv2)


def _normalised_pair(acc_even, acc_odd, low):
    num = jnp.where(low, acc_even, acc_odd)
    den = pltpu.roll(jnp.where(low, acc_odd, acc_even), V_DIM, 1)
    return num / den


def _mla_kernel(qn_ref, qr_ref, kn_ref, krt_ref, v_ref, o_ref, ve_ref, vo_ref, *, n_tiles):
    t = ATT_TILE
    lane = _lane_iota()
    low = lane < V_DIM
    v = v_ref[...]
    one = jnp.ones_like(v)
    ve_ref[...] = jnp.where(low, v, one)
    vo_ref[...] = jnp.where(low, one, v)
    rope_base = (pl.program_id(1) % (ROPE_PER_BLOCK // 2)) * 2 * ROPE_DIM
    row_chunk = lax.broadcasted_iota(jnp.int32, (t, t), 0) // CHUNK
    col_chunk = lax.broadcasted_iota(jnp.int32, (t, t), 1) // CHUNK
    causal = col_chunk <= row_chunk
    causal = jnp.concatenate([causal, causal], axis=0)

    for qi in range(n_tiles):
        r0, n_keys = qi * t, (qi + 1) * t
        qn = qn_ref[r0:r0 + t, :]
        qr = qr_ref[r0:r0 + t, :]
        k2 = jnp.concatenate([kn_ref[0:n_keys, :], krt_ref[0:n_keys, :]], axis=1)
        q_heads = []
        for hh in range(2):
            nope_mask = low if hh == 0 else jnp.logical_not(low)
            lo = rope_base + hh * ROPE_DIM
            rope_mask = jnp.logical_and(lane >= lo, lane < lo + ROPE_DIM)
            q_heads.append(jnp.concatenate(
                [jnp.where(nope_mask, qn, jnp.zeros_like(qn)),
                 jnp.where(rope_mask, qr, jnp.zeros_like(qr))], axis=1))
        s = _dot_nt(jnp.concatenate(q_heads, axis=0), k2)
        s_diag = jnp.where(causal, s[:, r0:], NEG_INF)
        s = s_diag if qi == 0 else jnp.concatenate([s[:, :r0], s_diag], axis=1)
        p = jnp.exp2(s - jnp.max(s, axis=1, keepdims=True)).astype(BF16)
        o_ref[r0:r0 + t, :] = _normalised_pair(
            _dot(p[:t], ve_ref[0:n_keys, :]), _dot(p[t:], vo_ref[0:n_keys, :]), low)


def _mla_prompt(qn, qr, kn, krt, v, *, seq_len):
    n = qn.shape[0]
    assert seq_len % ATT_TILE == 0
    n_pairs = H_A // 2
    blk = lambda f: pl.BlockSpec((seq_len, LANES), f)
    return pl.pallas_call(
        functools.partial(_mla_kernel, n_tiles=seq_len // ATT_TILE),
        grid=(n // seq_len, n_pairs),
        in_specs=[
            blk(lambda b, j: (b, j)),
            blk(lambda b, j: (b, j // (ROPE_PER_BLOCK // 2))),
            blk(lambda b, j: (b, j)),
            blk(lambda b, j: (b, 0)),
            blk(lambda b, j: (b, j)),
        ],
        out_specs=blk(lambda b, j: (b, j)),
        out_shape=jax.ShapeDtypeStruct((n, MIX_A), F32),
        scratch_shapes=[pltpu.VMEM((seq_len, LANES), BF16), pltpu.VMEM((seq_len, LANES), BF16)],
        compiler_params=_compiler_params(("arbitrary", "arbitrary")),
        name="mla_prompt",
    )(qn, qr, kn, krt, v)


def _band_kernel(q_ref, k_ref, v_ref, t_ref, o_ref, ve_ref, vo_ref, *, n_tiles):
    t = ATT_TILE
    lane = _lane_iota()
    low = lane < HD_B
    v = v_ref[...]
    one = jnp.ones_like(v)
    ve_ref[...] = jnp.where(low, v, one)
    vo_ref[...] = jnp.where(low, one, v)

    for g in range(n_tiles):
        r0 = g * t
        k_lo = max(g - (BAND_TILES - 1), 0) * t
        n_keys = r0 + t - k_lo
        q = q_ref[r0:r0 + t, :]
        zero_q = jnp.zeros_like(q)
        q2 = jnp.concatenate([jnp.where(low, q, zero_q), jnp.where(low, zero_q, q)], axis=0)
        bias = jnp.concatenate([t_ref[0, :, BAND_TILES * t - n_keys:],
                                t_ref[1, :, BAND_TILES * t - n_keys:]], axis=0)
        s = _dot_nt(q2, k_ref[k_lo:r0 + t, :]) + bias
        p = jnp.exp2(s - jnp.max(s, axis=1, keepdims=True)).astype(BF16)
        o_ref[r0:r0 + t, :] = _normalised_pair(
            _dot(p[:t], ve_ref[k_lo:r0 + t, :]), _dot(p[t:], vo_ref[k_lo:r0 + t, :]), low)


def _band_prompt(qb, kb, vb, table, *, seq_len):
    n = qb.shape[0]
    assert seq_len % ATT_TILE == 0
    blk = pl.BlockSpec((seq_len, LANES), lambda b, j: (b, j))
    return pl.pallas_call(
        functools.partial(_band_kernel, n_tiles=seq_len // ATT_TILE),
        grid=(n // seq_len, H_B // 2),
        in_specs=[blk, blk, blk,
                  pl.BlockSpec((2,) + table.shape[1:], lambda b, j: (j, 0, 0))],
        out_specs=blk,
        out_shape=jax.ShapeDtypeStruct((n, MIX_B), F32),
        scratch_shapes=[pltpu.VMEM((seq_len, LANES), BF16), pltpu.VMEM((seq_len, LANES), BF16)],
        compiler_params=_compiler_params(("arbitrary", "arbitrary")),
        name="band_prompt",
    )(qb, kb, vb, table)


def _sample_kernel(qn_ref, qr_ref, ckvn_ref, krtn_ref, qb_ref, kbn_ref, vbn_ref,
                   cckv_ref, ckrt_ref, cbkt_ref, cbvt_ref, t_ref, wukt_ref, wuv_ref,
                   oa_ref, ob_ref, *, n_tok):
    lane = _lane_iota()
    low = lane < V_DIM

    qn = qn_ref[...]
    qr = qr_ref[...]
    q_lat, q_rope = [], []
    for h in range(H_A):
        pair, par = divmod(h, 2)
        blk = qn[:, pair * LANES:(pair + 1) * LANES]
        head_mask = low if par == 0 else jnp.logical_not(low)
        q_h = jnp.where(head_mask, blk, jnp.zeros_like(blk))
        q_lat.append(_dot(q_h, wukt_ref[pair * LANES:(pair + 1) * LANES, :]))
        rblk, rpos = divmod(h, ROPE_PER_BLOCK)
        r = qr[:, rblk * LANES:(rblk + 1) * LANES]
        own = jnp.logical_and(lane >= rpos * ROPE_DIM, lane < (rpos + 1) * ROPE_DIM)
        q_rope.append(jnp.where(own, r, jnp.zeros_like(r)))
    q_lat = jnp.concatenate(q_lat, axis=0).astype(BF16)
    q_rope = jnp.concatenate(q_rope, axis=0)

    ckv_c = cckv_ref[0].astype(BF16)
    krt_c = jnp.concatenate([ckrt_ref[0].astype(BF16)] * ROPE_PER_BLOCK, axis=0)
    ckv_n = ckvn_ref[...].astype(BF16)
    s_c = _dot_nt(q_lat, ckv_c) + _dot(q_rope, krt_c)
    s_n = _dot_nt(q_lat, ckv_n) + _dot_nt(q_rope, krtn_ref[...])
    m = jnp.maximum(jnp.max(s_c, axis=1, keepdims=True), jnp.max(s_n, axis=1, keepdims=True))
    p_c = jnp.exp2(s_c - m)
    p_n = jnp.exp2(s_n - m)
    l = jnp.sum(p_c, axis=1, keepdims=True) + jnp.sum(p_n, axis=1, keepdims=True)
    o_lat = ((_dot(p_c.astype(BF16), ckv_c) + _dot(p_n.astype(BF16), ckv_n)) / l).astype(BF16)
    wuv = wuv_ref[...]
    col_head = lax.broadcasted_iota(jnp.int32, (1, MIX_A), 1) // V_DIM
    out_a = None
    for h in range(H_A):
        w_h = jnp.where(col_head == h, wuv, jnp.zeros_like(wuv))
        o_h = _dot(o_lat[h * n_tok:(h + 1) * n_tok], w_h)
        out_a = o_h if out_a is None else out_a + o_h
    oa_ref[...] = out_a

    n_cache = cbkt_ref.shape[2]
    qb = qb_ref[...]
    kb_n = kbn_ref[...]
    vb_n = vbn_ref[...]
    band_off = BAND_PAST - n_cache
    out_pairs = []
    for pair in range(H_B // 2):
        cols = slice(pair * LANES, (pair + 1) * LANES)
        blk = qb[:, cols]
        q2 = jnp.concatenate([jnp.where(low, blk, jnp.zeros_like(blk)),
                              jnp.where(low, jnp.zeros_like(blk), blk)], axis=0)
        bias = jnp.concatenate([t_ref[2 * pair], t_ref[2 * pair + 1]], axis=0)
        kt_c = cbkt_ref[0, cols, :].astype(BF16)
        vt_c = cbvt_ref[0, cols, :].astype(BF16)
        s_c = _dot(q2, kt_c) + bias[:, band_off:BAND_PAST]
        s_n = _dot_nt(q2, kb_n[:, cols]) + bias[:, BAND_PAST:BAND_PAST + n_tok]
        m = jnp.maximum(jnp.max(s_c, axis=1, keepdims=True), jnp.max(s_n, axis=1, keepdims=True))
        p_c = jnp.exp2(s_c - m)
        p_n = jnp.exp2(s_n - m)
        l = jnp.sum(p_c, axis=1, keepdims=True) + jnp.sum(p_n, axis=1, keepdims=True)
        o2 = (_dot_nt(p_c.astype(BF16), vt_c) + _dot(p_n.astype(BF16), vb_n[:, cols])) / l
        out_pairs.append(jnp.where(low, o2[0:n_tok], o2[n_tok:2 * n_tok]))
    ob_ref[...] = jnp.concatenate(out_pairs, axis=1)


def _sample_attention(proj, caches, table, wukt, wuv, *, n_streams, n_tok):
    qn, qr, _, krt, _, qb, kb, vb, ckv, _, _, _ = proj
    cache_ckv, cache_kr, cache_bk, cache_bv = caches
    n_band = cache_bk.shape[1]
    assert n_band <= BAND_PAST and BAND_PAST + n_tok <= table.shape[2]
    ckrt = jnp.swapaxes(cache_kr, 1, 2)
    cbkt = jnp.transpose(cache_bk, (0, 2, 3, 1)).reshape(n_streams, MIX_B, n_band)
    cbvt = jnp.transpose(cache_bv, (0, 2, 3, 1)).reshape(n_streams, MIX_B, n_band)
    tok = lambda width: pl.BlockSpec((n_tok, width), lambda b: (b, 0))
    per_stream = lambda a: pl.BlockSpec((1,) + a.shape[1:], lambda b: (b, 0, 0))
    out = jax.ShapeDtypeStruct((n_streams * n_tok, MIX_A), F32)
    return pl.pallas_call(
        functools.partial(_sample_kernel, n_tok=n_tok),
        grid=(n_streams,),
        in_specs=[
            tok(qn.shape[1]), tok(qr.shape[1]), tok(ckv.shape[1]), tok(krt.shape[1]),
            tok(qb.shape[1]), tok(kb.shape[1]), tok(vb.shape[1]),
            per_stream(cache_ckv), per_stream(ckrt), per_stream(cbkt), per_stream(cbvt),
            pl.BlockSpec((H_B, n_tok, table.shape[2]), lambda b: (0, 0, 0)),
            _const_spec(wukt.shape), _const_spec(wuv.shape),
        ],
        out_specs=(tok(MIX_A), tok(MIX_B)),
        out_shape=(out, out),
        compiler_params=_compiler_params(("arbitrary",)),
        name="sample_attention",
    )(qn, qr, ckv, krt, qb, kb, vb, cache_ckv, ckrt, cbkt, cbvt, table, wukt, wuv)


def _ffn_chunks(d_ff):
    chunks, start = [], 0
    while start < d_ff:
        size = min(2 * MXU_DIM, d_ff - start)
        chunks.append((start, size))
        start += size
    return chunks


def _out_kernel(x_ref, oa_ref, ob_ref, goa_ref, gob_ref, gffn_ref, gfin_ref,
                wout_ref, wg_ref, wu_ref, wd_ref, y_ref):
    mix = jnp.concatenate([_rms(oa_ref[...], goa_ref[...]), _rms(ob_ref[...], gob_ref[...])],
                          axis=1).astype(BF16)
    x1 = x_ref[...] + _dot(mix, wout_ref[...])
    h = _rms(x1, gffn_ref[...]).astype(BF16)
    ffn = None
    for start, size in _ffn_chunks(wg_ref.shape[1]):
        gate = _dot(h, wg_ref[:, start:start + size])
        up = _dot(h, wu_ref[:, start:start + size])
        act = (jax.nn.silu(gate) * up).astype(BF16)
        part = _dot(act, wd_ref[start:start + size, :])
        ffn = part if ffn is None else ffn + part
    y_ref[...] = _rms(x1 + ffn, gfin_ref[...])


def _output(x2d, oa, ob, weights, *, tm):
    n, d = x2d.shape
    assert n % tm == 0
    row = lambda width: pl.BlockSpec((tm, width), lambda i: (i, 0))
    return pl.pallas_call(
        _out_kernel,
        grid=(n // tm,),
        in_specs=[row(d), row(MIX_A), row(MIX_B)] + [_const_spec(w.shape) for w in weights],
        out_specs=row(d),
        out_shape=jax.ShapeDtypeStruct((n, d), F32),
        compiler_params=_compiler_params(("arbitrary",)),
        name="output_ffn",
    )(x2d, oa, ob, *weights)


def _rotate_half_cols(w):
    half = ROPE_DIM // 2
    return jnp.concatenate([-w[..., half:], w[..., :half]], axis=-1)


def _layout_weights(w_in, w_uq, w_uk, w_uv):
    d = w_in.shape[0]
    o = 0
    w_cq = w_in[:, o:o + Q_RANK]; o += Q_RANK
    w_ckv = w_in[:, o:o + KV_RANK]; o += KV_RANK
    w_kr = w_in[:, o:o + ROPE_DIM]; o += ROPE_DIM
    w_qb = w_in[:, o:o + MIX_B]; o += MIX_B
    w_kvb = w_in[:, o:]
    w1 = jnp.concatenate(
        [w_cq, w_ckv, jnp.tile(w_kr, (1, ROPE_PER_BLOCK)),
         jnp.tile(_rotate_half_cols(w_kr), (1, ROPE_PER_BLOCK)),
         w_qb, w_kvb], axis=1).astype(BF16)
    uq = w_uq.reshape(Q_RANK, H_A, NOPE_DIM + ROPE_DIM)
    uq_rope = uq[:, :, NOPE_DIM:]
    wq2 = jnp.concatenate(
        [uq[:, :, :NOPE_DIM].reshape(Q_RANK, H_A * NOPE_DIM),
         uq_rope.reshape(Q_RANK, H_A * ROPE_DIM),
         _rotate_half_cols(uq_rope).reshape(Q_RANK, H_A * ROPE_DIM)], axis=1).astype(BF16)
    uk = w_uk.reshape(KV_RANK, H_A * NOPE_DIM)
    uv = w_uv.reshape(KV_RANK, H_A * V_DIM)
    wkv2 = jnp.concatenate([uk, uv], axis=1).astype(BF16)
    return w1, wq2, wkv2, uk.T.astype(BF16), uv.astype(BF16)


def _rope_table(pos):
    inv = ROPE_THETA ** (-jnp.arange(0, ROPE_DIM, 2, dtype=F32) / ROPE_DIM)
    ang = pos.astype(F32)[:, None] * inv[None, :]
    cos = jnp.tile(jnp.cos(ang), (1, 2 * ROPE_PER_BLOCK))
    sin = jnp.tile(jnp.sin(ang), (1, 2 * ROPE_PER_BLOCK))
    return jnp.concatenate([cos, sin], axis=1)


def _row_tile(n, cap):
    tm = min(n, cap)
    assert n % tm == 0
    return tm


def kernel(x_prompt, x_sample, cache_mla_ckv, cache_mla_krope, cache_band_k, cache_band_v,
           w_in, g_attn, g_q, w_uq, g_kv, w_uk, w_uv, rel_bias, g_out_a, g_out_b, w_out,
           g_ffn, w_gate, w_up, w_down, g_final):
    depth = w_in.shape[0]
    assert depth == 1, "single-layer trunk"
    batch, seq, d = x_prompt.shape
    n_streams, n_tok, _ = x_sample.shape
    past = cache_mla_ckv.shape[2]

    w1, wq2, wkv2, wukt, wuv = _layout_weights(w_in[0], w_uq[0], w_uk[0], w_uv[0])
    proj_w = (g_attn, g_q, g_kv, w1, wq2, wkv2)
    out_w = (g_out_a, g_out_b, g_ffn, g_final[None, :], w_out[0].astype(BF16),
             w_gate[0].astype(BF16), w_up[0].astype(BF16), w_down[0].astype(BF16))
    table = _bias_table(rel_bias[0])

    n_keep = min(BAND_PAST, seq)
    xp = x_prompt.reshape(batch * seq, d)
    tm = _row_tile(n_keep, 512)
    proj = _project(xp, _rope_table(jnp.arange(seq, dtype=jnp.int32)), proj_w,
                    seq_len=seq, n_keep=n_keep, tm=tm, seq_minor=True)
    qn, qr, kn, krt, v, qb, kb, vb, ckv_p, krt_p, bkt_p, bvt_p = proj
    kr_p = jnp.swapaxes(krt_p, 1, 2)
    band_state = lambda s: jnp.transpose(s.reshape(batch, H_B, HD_B, n_keep), (0, 3, 1, 2))
    bk_p, bv_p = band_state(bkt_p), band_state(bvt_p)
    oa = _mla_prompt(qn, qr, kn, krt, v, seq_len=seq)
    ob = _band_prompt(qb, kb, vb, table, seq_len=seq)
    y_prompt = _output(xp, oa, ob, out_w, tm=tm).reshape(batch, seq, d)

    n_s = n_streams * n_tok
    xs = x_sample.reshape(n_s, d)
    pos_s = past + jnp.tile(jnp.arange(n_tok, dtype=jnp.int32), n_streams)
    proj_s = _project(xs, _rope_table(pos_s), proj_w, seq_len=n_s, n_keep=n_s, tm=n_s,
                      seq_minor=False)
    oa_s, ob_s = _sample_attention(
        proj_s, (cache_mla_ckv[0], cache_mla_krope[0], cache_band_k[0], cache_band_v[0]),
        table, wukt, wuv, n_streams=n_streams, n_tok=n_tok)
    y_sample = _output(xs, oa_s, ob_s, out_w, tm=n_s).reshape(n_streams, n_tok, d)
    ckv_s, kr_s, bk_s, bv_s = proj_s[8:12]

    return (
        y_prompt, y_sample,
        ckv_p.reshape(1, batch, seq, KV_RANK), kr_p.reshape(1, batch, seq, ROPE_DIM),
        bk_p.reshape(1, batch, n_keep, H_B, HD_B), bv_p.reshape(1, batch, n_keep, H_B, HD_B),
        ckv_s.reshape(1, n_streams, n_tok, KV_RANK), kr_s.reshape(1, n_streams, n_tok, ROPE_DIM),
        bk_s.reshape(1, n_streams, n_tok, H_B, HD_B), bv_s.reshape(1, n_streams, n_tok, H_B, HD_B),
    )
```

```python
import functools

import jax
import jax.numpy as jnp
from jax import lax
from jax.experimental import pallas as pl
from jax.experimental.pallas import tpu as pltpu

CHUNK = 64
EPS = 1e-6
NEG_INF = -1e30
H_A = 8
NOPE_DIM = 64
ROPE_DIM = 32
V_DIM = 64
Q_RANK = 256
KV_RANK = 256
ROPE_THETA = 10000.0
MLA_SCALE = (NOPE_DIM + ROPE_DIM) ** -0.5
H_B = 8
HD_B = 64
N_PREV_CHUNKS = 8
BAND_PAST = N_PREV_CHUNKS * CHUNK
REL_CLIP = 256
BAND_SCALE = HD_B ** -0.5
LOG2E = 1.4426950408889634
MIX_A = H_A * V_DIM
MIX_B = H_B * HD_B

LANES = 128
MXU_DIM = 256
VMEM_LIMIT_BYTES = 56 * 1024 * 1024

ATT_TILE = 4 * CHUNK
BAND_TILES = N_PREV_CHUNKS * CHUNK // ATT_TILE + 1
PAIR = 2 * V_DIM
ROPE_PER_BLOCK = LANES // ROPE_DIM

F32 = jnp.float32
BF16 = jnp.bfloat16


def _dot(a, b):
    return jnp.dot(a, b, preferred_element_type=F32)


def _dot_nt(a, b):
    return lax.dot_general(a, b, (((1,), (1,)), ((), ())), preferred_element_type=F32)


def _rms(x, g):
    return x * lax.rsqrt(jnp.mean(x * x, axis=-1, keepdims=True) + EPS) * g


def _lane_iota(width=LANES):
    return lax.broadcasted_iota(jnp.int32, (1, width), 1)


def _compiler_params(semantics):
    return pltpu.CompilerParams(dimension_semantics=semantics, vmem_limit_bytes=VMEM_LIMIT_BYTES)


def _const_spec(shape):
    nd = len(shape)
    return pl.BlockSpec(shape, lambda *_: (0,) * nd, pipeline_mode=pl.Buffered(1))


def _bias_kernel(g_ref, t_ref, tk_ref):
    rows, width = t_ref.shape[1], g_ref.shape[2]
    x = jnp.broadcast_to(g_ref[0], (rows, width))
    x = pltpu.roll(x, 0, 1, stride=1, stride_axis=0)
    x = x[:, width - t_ref.shape[2]:]
    qc = lax.broadcasted_iota(jnp.int32, x.shape, 0) // CHUNK
    kc = lax.broadcasted_iota(jnp.int32, x.shape, 1) // CHUNK
    visible = jnp.logical_and(kc >= qc, kc <= qc + N_PREV_CHUNKS)
    table = jnp.where(visible, x * LOG2E, NEG_INF)
    t_ref[0] = table
    tk_ref[0] = table.T


def _bias_table(rel_bias):
    n_keys = BAND_TILES * ATT_TILE
    width = n_keys + ATT_TILE
    n_const = width - 2 * REL_CLIP + 1
    g = jnp.concatenate(
        [jnp.broadcast_to(rel_bias[:, 2 * REL_CLIP:], (H_B, n_const)),
         rel_bias[:, 2 * REL_CLIP - 1:0:-1]], axis=1)
    g = g.reshape(H_B, 1, width)
    return pl.pallas_call(
        _bias_kernel,
        grid=(H_B,),
        in_specs=[pl.BlockSpec((1, 1, width), lambda h: (h, 0, 0))],
        out_specs=(pl.BlockSpec((1, ATT_TILE, n_keys), lambda h: (h, 0, 0)),
                   pl.BlockSpec((1, n_keys, ATT_TILE), lambda h: (h, 0, 0))),
        out_shape=(jax.ShapeDtypeStruct((H_B, ATT_TILE, n_keys), F32),
                   jax.ShapeDtypeStruct((H_B, n_keys, ATT_TILE), F32)),
        compiler_params=_compiler_params(("arbitrary",)),
        name="bias_table",
    )(g)


def _proj_kernel(x_ref, tab_ref, ga_ref, gq_ref, gkv_ref, w1_ref, wq2_ref, wkv2_ref,
                 qn_ref, qr_ref, kn_ref, krt_ref, v_ref, qb_ref, kb_ref, vb_ref,
                 ckv_ref, kr_ref, bk_ref, bv_ref, *, tiles_per_seq, keep_from_tile, seq_minor):
    xn = _rms(x_ref[...], ga_ref[...]).astype(BF16)
    cos = tab_ref[:, 0:LANES]
    sin = tab_ref[:, LANES:2 * LANES]

    c_q = _dot(xn, w1_ref[:, 0:Q_RANK])
    cqn = _rms(c_q, gq_ref[...]).astype(BF16)
    q2 = _dot(cqn, wq2_ref[...])
    n_nope = H_A * NOPE_DIM
    n_rope = H_A * ROPE_DIM
    qn_ref[...] = (q2[:, 0:n_nope] * (MLA_SCALE * LOG2E)).astype(BF16)
    cos2 = jnp.concatenate([cos] * (n_rope // LANES), axis=1)
    sin2 = jnp.concatenate([sin] * (n_rope // LANES), axis=1)
    q_rope = q2[:, n_nope:n_nope + n_rope] * cos2 + q2[:, n_nope + n_rope:] * sin2
    qr_ref[...] = (q_rope * (MLA_SCALE * LOG2E)).astype(BF16)

    o = Q_RANK
    c_kv = _dot(xn, w1_ref[:, o:o + KV_RANK])
    ckvn = _rms(c_kv, gkv_ref[...])
    ckv_ref[...] = ckvn
    kv2 = _dot(ckvn.astype(BF16), wkv2_ref[...])
    kn_ref[...] = kv2[:, 0:n_nope].astype(BF16)
    v_ref[...] = kv2[:, n_nope:].astype(BF16)

    o += KV_RANK
    krr = _dot(xn, w1_ref[:, o:o + 2 * LANES])
    k_rope = krr[:, 0:LANES] * cos + krr[:, LANES:] * sin
    krt_ref[...] = k_rope.astype(BF16)
    if seq_minor:
        kr_ref[0] = k_rope.T[0:ROPE_DIM, :]
    else:
        kr_ref[...] = k_rope[:, 0:ROPE_DIM]

    o += 2 * LANES
    qkv = _dot(xn, w1_ref[:, o:o + 3 * MIX_B])
    qb_ref[...] = (qkv[:, 0:MIX_B] * (BAND_SCALE * LOG2E)).astype(BF16)
    kb = qkv[:, MIX_B:2 * MIX_B]
    vb = qkv[:, 2 * MIX_B:]
    kb_ref[...] = kb.astype(BF16)
    vb_ref[...] = vb.astype(BF16)

    @pl.when(pl.program_id(0) % tiles_per_seq >= keep_from_tile)
    def _():
        bk_ref[0] = kb.T if seq_minor else kb
        bv_ref[0] = vb.T if seq_minor else vb


def _project(x2d, tab, weights, *, seq_len, n_keep, tm, seq_minor):
    n, d = x2d.shape
    g_attn, g_q, g_kv, w1, wq2, wkv2 = weights
    assert n % seq_len == 0 and seq_len % tm == 0 and n_keep % tm == 0
    tiles_per_seq = seq_len // tm
    keep_from_tile = (seq_len - n_keep) // tm
    n_seq = n // seq_len
    keep_tiles = n_keep // tm

    row = lambda width: pl.BlockSpec((tm, width), lambda i: (i, 0))
    keep_tile = lambda i: jnp.maximum(i % tiles_per_seq - keep_from_tile, 0)
    if seq_minor:
        kr_spec = pl.BlockSpec((1, ROPE_DIM, tm), lambda i: (i // tiles_per_seq, 0, i % tiles_per_seq))
        kr_shape = jax.ShapeDtypeStruct((n_seq, ROPE_DIM, seq_len), F32)
        keep_spec = pl.BlockSpec((1, MIX_B, tm), lambda i: (i // tiles_per_seq, 0, keep_tile(i)))
        keep_shape = jax.ShapeDtypeStruct((n_seq, MIX_B, keep_tiles * tm), F32)
    else:
        kr_spec = row(ROPE_DIM)
        kr_shape = jax.ShapeDtypeStruct((n, ROPE_DIM), F32)
        keep_spec = pl.BlockSpec((1, tm, MIX_B), lambda i: (i // tiles_per_seq, keep_tile(i), 0))
        keep_shape = jax.ShapeDtypeStruct((n_seq, keep_tiles * tm, MIX_B), F32)
    bf = lambda width: jax.ShapeDtypeStruct((n, width), BF16)
    out_shape = (
        bf(H_A * NOPE_DIM), bf(H_A * ROPE_DIM), bf(H_A * NOPE_DIM), bf(LANES), bf(MIX_A),
        bf(MIX_B), bf(MIX_B), bf(MIX_B),
        jax.ShapeDtypeStruct((n, KV_RANK), F32), kr_shape, keep_shape, keep_shape,
    )
    out_specs = (
        row(H_A * NOPE_DIM), row(H_A * ROPE_DIM), row(H_A * NOPE_DIM), row(LANES), row(MIX_A),
        row(MIX_B), row(MIX_B), row(MIX_B), row(KV_RANK), kr_spec, keep_spec, keep_spec,
    )
    in_specs = [
        row(d),
        pl.BlockSpec((tm, 2 * LANES), lambda i: (i % tiles_per_seq, 0)),
        _const_spec(g_attn.shape), _const_spec(g_q.shape), _const_spec(g_kv.shape),
        _const_spec(w1.shape), _const_spec(wq2.shape), _const_spec(wkv2.shape),
    ]
    return pl.pallas_call(
        functools.partial(_proj_kernel, tiles_per_seq=tiles_per_seq, keep_from_tile=keep_from_tile,
                          seq_minor=seq_minor),
        grid=(n // tm,),
        in_specs=in_specs,
        out_specs=out_specs,
        out_shape=out_shape,
        compiler_params=_compiler_params(("arbitrary",)),
        name="projection",
    )(x2d, tab, g_attn, g_q, g_kv, w1, wq2, wkv2)


def _normalised_pair(acc_even, acc_odd, low):
    num = jnp.where(low, acc_even, acc_odd)
    den = pltpu.roll(jnp.where(low, acc_odd, acc_even), V_DIM, 1)
    return num / den


def _store_value_rows(v_ref, ve_ref, vo_ref):
    vt = v_ref[...].astype(F32).T
    own_even = lax.broadcasted_iota(jnp.int32, (LANES, 1), 0) < V_DIM
    ve_ref[...] = jnp.where(own_even, vt, 1.0).astype(BF16)
    vo_ref[...] = jnp.where(own_even, 1.0, vt).astype(BF16)


def _pipelined_softmax(tiles, scores, attend):
    s_next = scores(tiles[0])
    pending = None
    for i, tile in enumerate(tiles):
        s = s_next
        if i + 1 < len(tiles):
            s_next = scores(tiles[i + 1])
        p = jnp.exp2(s - jnp.max(s, axis=0, keepdims=True)).astype(BF16)
        if pending is not None:
            attend(*pending)
        pending = (tile, p)
    attend(*pending)


def _normalised_pair_t(acc_even, acc_odd):
    own_even = lax.broadcasted_iota(jnp.int32, (LANES, 1), 0) < V_DIM
    num = jnp.where(own_even, acc_even, acc_odd)
    sums = jnp.where(own_even, acc_odd, acc_even)
    den = jnp.concatenate([sums[V_DIM:], sums[:V_DIM]], axis=0)
    return (num / den).T


def _mla_kernel(qn_ref, qr_ref, kn_ref, krt_ref, v_ref, o_ref, ve_ref, vo_ref, *, n_tiles):
    t = ATT_TILE
    lane = _lane_iota()
    low = lane < V_DIM
    _store_value_rows(v_ref, ve_ref, vo_ref)
    rope_base = (pl.program_id(1) % (ROPE_PER_BLOCK // 2)) * 2 * ROPE_DIM
    key_chunk = lax.broadcasted_iota(jnp.int32, (t, 2 * t), 0) // CHUNK
    query_chunk = (lax.broadcasted_iota(jnp.int32, (t, 2 * t), 1) % t) // CHUNK
    causal = key_chunk <= query_chunk

    def scores(qi):
        r0, n_keys = qi * t, (qi + 1) * t
        qn = qn_ref[r0:r0 + t, :]
        qr = qr_ref[r0:r0 + t, :]
        k2 = jnp.concatenate([kn_ref[0:n_keys, :], krt_ref[0:n_keys, :]], axis=1)
        q_heads = []
        for hh in range(2):
            nope_mask = low if hh == 0 else jnp.logical_not(low)
            lo = rope_base + hh * ROPE_DIM
            rope_mask = jnp.logical_and(lane >= lo, lane < lo + ROPE_DIM)
            q_heads.append(jnp.concatenate(
                [jnp.where(nope_mask, qn, jnp.zeros_like(qn)),
                 jnp.where(rope_mask, qr, jnp.zeros_like(qr))], axis=1))
        s = _dot_nt(k2, jnp.concatenate(q_heads, axis=0))
        s_diag = jnp.where(causal, s[r0:, :], NEG_INF)
        return s_diag if qi == 0 else jnp.concatenate([s[:r0, :], s_diag], axis=0)

    def attend(qi, p):
        r0, n_keys = qi * t, (qi + 1) * t
        o_ref[r0:r0 + t, :] = _normalised_pair_t(
            _dot(ve_ref[:, 0:n_keys], p[:, :t]), _dot(vo_ref[:, 0:n_keys], p[:, t:]))

    _pipelined_softmax(list(range(n_tiles)), scores, attend)


def _mla_prompt(qn, qr, kn, krt, v, *, seq_len):
    n = qn.shape[0]
    assert seq_len % ATT_TILE == 0
    n_pairs = H_A // 2
    blk = lambda f: pl.BlockSpec((seq_len, LANES), f)
    return pl.pallas_call(
        functools.partial(_mla_kernel, n_tiles=seq_len // ATT_TILE),
        grid=(n // seq_len, n_pairs),
        in_specs=[
            blk(lambda b, j: (b, j)),
            blk(lambda b, j: (b, j // (ROPE_PER_BLOCK // 2))),
            blk(lambda b, j: (b, j)),
            blk(lambda b, j: (b, 0)),
            blk(lambda b, j: (b, j)),
        ],
        out_specs=blk(lambda b, j: (b, j)),
        out_shape=jax.ShapeDtypeStruct((n, MIX_A), F32),
        scratch_shapes=[pltpu.VMEM((LANES, seq_len), BF16), pltpu.VMEM((LANES, seq_len), BF16)],
        compiler_params=_compiler_params(("arbitrary", "arbitrary")),
        name="mla_prompt",
    )(qn, qr, kn, krt, v)


def _band_kernel(q_ref, k_ref, v_ref, t_ref, o_ref, ve_ref, vo_ref, *, n_tiles):
    t = ATT_TILE
    low = _lane_iota() < HD_B
    _store_value_rows(v_ref, ve_ref, vo_ref)

    def window(g):
        return max(g - (BAND_TILES - 1), 0) * t, (g + 1) * t

    def scores(g):
        k_lo, k_hi = window(g)
        q = q_ref[g * t:(g + 1) * t, :]
        zero_q = jnp.zeros_like(q)
        q2 = jnp.concatenate([jnp.where(low, q, zero_q), jnp.where(low, zero_q, q)], axis=0)
        first = BAND_TILES * t - (k_hi - k_lo)
        bias = jnp.concatenate([t_ref[0, first:, :], t_ref[1, first:, :]], axis=1)
        return _dot_nt(k_ref[k_lo:k_hi, :], q2) + bias

    def attend(g, p):
        k_lo, k_hi = window(g)
        o_ref[g * t:(g + 1) * t, :] = _normalised_pair_t(
            _dot(ve_ref[:, k_lo:k_hi], p[:, :t]), _dot(vo_ref[:, k_lo:k_hi], p[:, t:]))

    _pipelined_softmax(list(range(n_tiles)), scores, attend)


def _band_prompt(qb, kb, vb, table, *, seq_len):
    n = qb.shape[0]
    assert seq_len % ATT_TILE == 0
    blk = pl.BlockSpec((seq_len, LANES), lambda b, j: (b, j))
    return pl.pallas_call(
        functools.partial(_band_kernel, n_tiles=seq_len // ATT_TILE),
        grid=(n // seq_len, H_B // 2),
        in_specs=[blk, blk, blk,
                  pl.BlockSpec((2,) + table.shape[1:], lambda b, j: (j, 0, 0))],
        out_specs=blk,
        out_shape=jax.ShapeDtypeStruct((n, MIX_B), F32),
        scratch_shapes=[pltpu.VMEM((LANES, seq_len), BF16), pltpu.VMEM((LANES, seq_len), BF16)],
        compiler_params=_compiler_params(("arbitrary", "arbitrary")),
        name="band_prompt",
    )(qb, kb, vb, table)


def _sample_kernel(qn_ref, qr_ref, ckvn_ref, krtn_ref, qb_ref, kbn_ref, vbn_ref,
                   cckv_ref, ckrt_ref, cbkt_ref, cbvt_ref, t_ref, wukt_ref, wuv_ref,
                   oa_ref, ob_ref, *, n_tok):
    lane = _lane_iota()
    low = lane < V_DIM

    qn = qn_ref[...]
    qr = qr_ref[...]
    q_lat, q_rope = [], []
    for h in range(H_A):
        pair, par = divmod(h, 2)
        blk = qn[:, pair * LANES:(pair + 1) * LANES]
        head_mask = low if par == 0 else jnp.logical_not(low)
        q_h = jnp.where(head_mask, blk, jnp.zeros_like(blk))
        q_lat.append(_dot(q_h, wukt_ref[pair * LANES:(pair + 1) * LANES, :]))
        rblk, rpos = divmod(h, ROPE_PER_BLOCK)
        r = qr[:, rblk * LANES:(rblk + 1) * LANES]
        own = jnp.logical_and(lane >= rpos * ROPE_DIM, lane < (rpos + 1) * ROPE_DIM)
        q_rope.append(jnp.where(own, r, jnp.zeros_like(r)))
    q_lat = jnp.concatenate(q_lat, axis=0).astype(BF16)
    q_rope = jnp.concatenate(q_rope, axis=0)

    ckv_c = cckv_ref[0].astype(BF16)
    krt_c = jnp.concatenate([ckrt_ref[0].astype(BF16)] * ROPE_PER_BLOCK, axis=0)
    ckv_n = ckvn_ref[...].astype(BF16)
    s_c = _dot_nt(q_lat, ckv_c) + _dot(q_rope, krt_c)
    s_n = _dot_nt(q_lat, ckv_n) + _dot_nt(q_rope, krtn_ref[...])
    m = jnp.maximum(jnp.max(s_c, axis=1, keepdims=True), jnp.max(s_n, axis=1, keepdims=True))
    p_c = jnp.exp2(s_c - m)
    p_n = jnp.exp2(s_n - m)
    l = jnp.sum(p_c, axis=1, keepdims=True) + jnp.sum(p_n, axis=1, keepdims=True)
    o_lat = ((_dot(p_c.astype(BF16), ckv_c) + _dot(p_n.astype(BF16), ckv_n)) / l).astype(BF16)
    wuv = wuv_ref[...]
    col_head = lax.broadcasted_iota(jnp.int32, (1, MIX_A), 1) // V_DIM
    out_a = None
    for h in range(H_A):
        w_h = jnp.where(col_head == h, wuv, jnp.zeros_like(wuv))
        o_h = _dot(o_lat[h * n_tok:(h + 1) * n_tok], w_h)
        out_a = o_h if out_a is None else out_a + o_h
    oa_ref[...] = out_a

    n_cache = cbkt_ref.shape[2]
    qb = qb_ref[...]
    kb_n = kbn_ref[...]
    vb_n = vbn_ref[...]
    band_off = BAND_PAST - n_cache
    out_pairs = []
    for pair in range(H_B // 2):
        cols = slice(pair * LANES, (pair + 1) * LANES)
        blk = qb[:, cols]
        q2 = jnp.concatenate([jnp.where(low, blk, jnp.zeros_like(blk)),
                              jnp.where(low, jnp.zeros_like(blk), blk)], axis=0)
        bias = jnp.concatenate([t_ref[2 * pair], t_ref[2 * pair + 1]], axis=0)
        kt_c = cbkt_ref[0, cols, :].astype(BF16)
        vt_c = cbvt_ref[0, cols, :].astype(BF16)
        s_c = _dot(q2, kt_c) + bias[:, band_off:BAND_PAST]
        s_n = _dot_nt(q2, kb_n[:, cols]) + bias[:, BAND_PAST:BAND_PAST + n_tok]
        m = jnp.maximum(jnp.max(s_c, axis=1, keepdims=True), jnp.max(s_n, axis=1, keepdims=True))
        p_c = jnp.exp2(s_c - m)
        p_n = jnp.exp2(s_n - m)
        l = jnp.sum(p_c, axis=1, keepdims=True) + jnp.sum(p_n, axis=1, keepdims=True)
        o2 = (_dot_nt(p_c.astype(BF16), vt_c) + _dot(p_n.astype(BF16), vb_n[:, cols])) / l
        out_pairs.append(jnp.where(low, o2[0:n_tok], o2[n_tok:2 * n_tok]))
    ob_ref[...] = jnp.concatenate(out_pairs, axis=1)


def _sample_attention(proj, caches, table, wukt, wuv, *, n_streams, n_tok):
    qn, qr, _, krt, _, qb, kb, vb, ckv, _, _, _ = proj
    cache_ckv, cache_kr, cache_bk, cache_bv = caches
    n_band = cache_bk.shape[1]
    assert n_band <= BAND_PAST and BAND_PAST + n_tok <= table.shape[2]
    ckrt = jnp.swapaxes(cache_kr, 1, 2)
    cbkt = jnp.transpose(cache_bk, (0, 2, 3, 1)).reshape(n_streams, MIX_B, n_band)
    cbvt = jnp.transpose(cache_bv, (0, 2, 3, 1)).reshape(n_streams, MIX_B, n_band)
    tok = lambda width: pl.BlockSpec((n_tok, width), lambda b: (b, 0))
    per_stream = lambda a: pl.BlockSpec((1,) + a.shape[1:], lambda b: (b, 0, 0))
    out = jax.ShapeDtypeStruct((n_streams * n_tok, MIX_A), F32)
    return pl.pallas_call(
        functools.partial(_sample_kernel, n_tok=n_tok),
        grid=(n_streams,),
        in_specs=[
            tok(qn.shape[1]), tok(qr.shape[1]), tok(ckv.shape[1]), tok(krt.shape[1]),
            tok(qb.shape[1]), tok(kb.shape[1]), tok(vb.shape[1]),
            per_stream(cache_ckv), per_stream(ckrt), per_stream(cbkt), per_stream(cbvt),
            pl.BlockSpec((H_B, n_tok, table.shape[2]), lambda b: (0, 0, 0)),
            _const_spec(wukt.shape), _const_spec(wuv.shape),
        ],
        out_specs=(tok(MIX_A), tok(MIX_B)),
        out_shape=(out, out),
        compiler_params=_compiler_params(("arbitrary",)),
        name="sample_attention",
    )(qn, qr, ckv, krt, qb, kb, vb, cache_ckv, ckrt, cbkt, cbvt, table, wukt, wuv)


def _ffn_chunks(d_ff):
    chunks, start = [], 0
    while start < d_ff:
        size = min(2 * MXU_DIM, d_ff - start)
        chunks.append((start, size))
        start += size
    return chunks


def _out_kernel(x_ref, oa_ref, ob_ref, goa_ref, gob_ref, gffn_ref, gfin_ref,
                wout_ref, wg_ref, wu_ref, wd_ref, y_ref):
    mix = jnp.concatenate([_rms(oa_ref[...], goa_ref[...]), _rms(ob_ref[...], gob_ref[...])],
                          axis=1).astype(BF16)
    x1 = x_ref[...] + _dot(mix, wout_ref[...])
    h = _rms(x1, gffn_ref[...]).astype(BF16)
    ffn = None
    for start, size in _ffn_chunks(wg_ref.shape[1]):
        gate = _dot(h, wg_ref[:, start:start + size])
        up = _dot(h, wu_ref[:, start:start + size])
        act = (jax.nn.silu(gate) * up).astype(BF16)
        part = _dot(act, wd_ref[start:start + size, :])
        ffn = part if ffn is None else ffn + part
    y_ref[...] = _rms(x1 + ffn, gfin_ref[...])


def _output(x2d, oa, ob, weights, *, tm):
    n, d = x2d.shape
    assert n % tm == 0
    row = lambda width: pl.BlockSpec((tm, width), lambda i: (i, 0))
    return pl.pallas_call(
        _out_kernel,
        grid=(n // tm,),
        in_specs=[row(d), row(MIX_A), row(MIX_B)] + [_const_spec(w.shape) for w in weights],
        out_specs=row(d),
        out_shape=jax.ShapeDtypeStruct((n, d), F32),
        compiler_params=_compiler_params(("arbitrary",)),
        name="output_ffn",
    )(x2d, oa, ob, *weights)


def _rotate_half_cols(w):
    half = ROPE_DIM // 2
    return jnp.concatenate([-w[..., half:], w[..., :half]], axis=-1)


def _layout_weights(w_in, w_uq, w_uk, w_uv):
    d = w_in.shape[0]
    o = 0
    w_cq = w_in[:, o:o + Q_RANK]; o += Q_RANK
    w_ckv = w_in[:, o:o + KV_RANK]; o += KV_RANK
    w_kr = w_in[:, o:o + ROPE_DIM]; o += ROPE_DIM
    w_qb = w_in[:, o:o + MIX_B]; o += MIX_B
    w_kvb = w_in[:, o:]
    w1 = jnp.concatenate(
        [w_cq, w_ckv, jnp.tile(w_kr, (1, ROPE_PER_BLOCK)),
         jnp.tile(_rotate_half_cols(w_kr), (1, ROPE_PER_BLOCK)),
         w_qb, w_kvb], axis=1).astype(BF16)
    uq = w_uq.reshape(Q_RANK, H_A, NOPE_DIM + ROPE_DIM)
    uq_rope = uq[:, :, NOPE_DIM:]
    wq2 = jnp.concatenate(
        [uq[:, :, :NOPE_DIM].reshape(Q_RANK, H_A * NOPE_DIM),
         uq_rope.reshape(Q_RANK, H_A * ROPE_DIM),
         _rotate_half_cols(uq_rope).reshape(Q_RANK, H_A * ROPE_DIM)], axis=1).astype(BF16)
    uk = w_uk.reshape(KV_RANK, H_A * NOPE_DIM)
    uv = w_uv.reshape(KV_RANK, H_A * V_DIM)
    wkv2 = jnp.concatenate([uk, uv], axis=1).astype(BF16)
    return w1, wq2, wkv2, uk.T.astype(BF16), uv.astype(BF16)


def _rope_table(pos):
    inv = ROPE_THETA ** (-jnp.arange(0, ROPE_DIM, 2, dtype=F32) / ROPE_DIM)
    ang = pos.astype(F32)[:, None] * inv[None, :]
    cos = jnp.tile(jnp.cos(ang), (1, 2 * ROPE_PER_BLOCK))
    sin = jnp.tile(jnp.sin(ang), (1, 2 * ROPE_PER_BLOCK))
    return jnp.concatenate([cos, sin], axis=1)


def _row_tile(n, cap):
    tm = min(n, cap)
    assert n % tm == 0
    return tm


def kernel(x_prompt, x_sample, cache_mla_ckv, cache_mla_krope, cache_band_k, cache_band_v,
           w_in, g_attn, g_q, w_uq, g_kv, w_uk, w_uv, rel_bias, g_out_a, g_out_b, w_out,
           g_ffn, w_gate, w_up, w_down, g_final):
    depth = w_in.shape[0]
    assert depth == 1, "single-layer trunk"
    batch, seq, d = x_prompt.shape
    n_streams, n_tok, _ = x_sample.shape
    past = cache_mla_ckv.shape[2]

    w1, wq2, wkv2, wukt, wuv = _layout_weights(w_in[0], w_uq[0], w_uk[0], w_uv[0])
    proj_w = (g_attn, g_q, g_kv, w1, wq2, wkv2)
    out_w = (g_out_a, g_out_b, g_ffn, g_final[None, :], w_out[0].astype(BF16),
             w_gate[0].astype(BF16), w_up[0].astype(BF16), w_down[0].astype(BF16))
    table, table_k = _bias_table(rel_bias[0])

    n_keep = min(BAND_PAST, seq)
    xp = x_prompt.reshape(batch * seq, d)
    tm = _row_tile(n_keep, 512)
    proj = _project(xp, _rope_table(jnp.arange(seq, dtype=jnp.int32)), proj_w,
                    seq_len=seq, n_keep=n_keep, tm=tm, seq_minor=True)
    qn, qr, kn, krt, v, qb, kb, vb, ckv_p, krt_p, bkt_p, bvt_p = proj
    kr_p = jnp.swapaxes(krt_p, 1, 2)
    band_state = lambda s: jnp.transpose(s.reshape(batch, H_B, HD_B, n_keep), (0, 3, 1, 2))
    bk_p, bv_p = band_state(bkt_p), band_state(bvt_p)
    oa = _mla_prompt(qn, qr, kn, krt, v, seq_len=seq)
    ob = _band_prompt(qb, kb, vb, table_k, seq_len=seq)
    y_prompt = _output(xp, oa, ob, out_w, tm=tm).reshape(batch, seq, d)

    n_s = n_streams * n_tok
    xs = x_sample.reshape(n_s, d)
    pos_s = past + jnp.tile(jnp.arange(n_tok, dtype=jnp.int32), n_streams)
    proj_s = _project(xs, _rope_table(pos_s), proj_w, seq_len=n_s, n_keep=n_s, tm=n_s,
                      seq_minor=False)
    oa_s, ob_s = _sample_attention(
        proj_s, (cache_mla_ckv[0], cache_mla_krope[0], cache_band_k[0], cache_band_v[0]),
        table, wukt, wuv, n_streams=n_streams, n_tok=n_tok)
    y_sample = _output(xs, oa_s, ob_s, out_w, tm=n_s).reshape(n_streams, n_tok, d)
    ckv_s, kr_s, bk_s, bv_s = proj_s[8:12]

    return (
        y_prompt, y_sample,
        ckv_p.reshape(1, batch, seq, KV_RANK), kr_p.reshape(1, batch, seq, ROPE_DIM),
        bk_p.reshape(1, batch, n_keep, H_B, HD_B), bv_p.reshape(1, batch, n_keep, H_B, HD_B),
        ckv_s.reshape(1, n_streams, n_tok, KV_RANK), kr_s.reshape(1, n_streams, n_tok, ROPE_DIM),
        bk_s.reshape(1, n_streams, n_tok, H_B, HD_B), bv_s.reshape(1, n_streams, n_tok, H_B, HD_B),
    )
```

```python
import functools

import jax
import jax.numpy as jnp
from jax import lax
from jax.experimental import pallas as pl
from jax.experimental.pallas import tpu as pltpu

CHUNK = 64
EPS = 1e-6
NEG_INF = -1e30
H_A = 8
NOPE_DIM = 64
ROPE_DIM = 32
V_DIM = 64
Q_RANK = 256
KV_RANK = 256
ROPE_THETA = 10000.0
MLA_SCALE = (NOPE_DIM + ROPE_DIM) ** -0.5
H_B = 8
HD_B = 64
N_PREV_CHUNKS = 8
BAND_PAST = N_PREV_CHUNKS * CHUNK
REL_CLIP = 256
BAND_SCALE = HD_B ** -0.5
LOG2E = 1.4426950408889634
MIX_A = H_A * V_DIM
MIX_B = H_B * HD_B

LANES = 128
MXU_DIM = 256
VMEM_LIMIT_BYTES = 56 * 1024 * 1024

ATT_TILE = 4 * CHUNK
BAND_TILES = N_PREV_CHUNKS * CHUNK // ATT_TILE + 1
PAIR = 2 * V_DIM
ROPE_PER_BLOCK = LANES // ROPE_DIM

F32 = jnp.float32
BF16 = jnp.bfloat16


def _dot(a, b):
    return jnp.dot(a, b, preferred_element_type=F32)


def _dot_nt(a, b):
    return lax.dot_general(a, b, (((1,), (1,)), ((), ())), preferred_element_type=F32)


def _rms(x, g):
    return x * lax.rsqrt(jnp.mean(x * x, axis=-1, keepdims=True) + EPS) * g


def _lane_iota(width=LANES):
    return lax.broadcasted_iota(jnp.int32, (1, width), 1)


def _compiler_params(semantics):
    return pltpu.CompilerParams(dimension_semantics=semantics, vmem_limit_bytes=VMEM_LIMIT_BYTES)


def _const_spec(shape):
    nd = len(shape)
    return pl.BlockSpec(shape, lambda *_: (0,) * nd, pipeline_mode=pl.Buffered(1))


def _bias_kernel(g_ref, t_ref, tk_ref):
    rows, width = t_ref.shape[1], g_ref.shape[2]
    x = jnp.broadcast_to(g_ref[0], (rows, width))
    x = pltpu.roll(x, 0, 1, stride=1, stride_axis=0)
    x = x[:, width - t_ref.shape[2]:]
    qc = lax.broadcasted_iota(jnp.int32, x.shape, 0) // CHUNK
    kc = lax.broadcasted_iota(jnp.int32, x.shape, 1) // CHUNK
    visible = jnp.logical_and(kc >= qc, kc <= qc + N_PREV_CHUNKS)
    table = jnp.where(visible, x * LOG2E, NEG_INF)
    t_ref[0] = table
    tk_ref[0] = table.T


def _bias_table(rel_bias):
    n_keys = BAND_TILES * ATT_TILE
    width = n_keys + ATT_TILE
    n_const = width - 2 * REL_CLIP + 1
    g = jnp.concatenate(
        [jnp.broadcast_to(rel_bias[:, 2 * REL_CLIP:], (H_B, n_const)),
         rel_bias[:, 2 * REL_CLIP - 1:0:-1]], axis=1)
    g = g.reshape(H_B, 1, width)
    return pl.pallas_call(
        _bias_kernel,
        grid=(H_B,),
        in_specs=[pl.BlockSpec((1, 1, width), lambda h: (h, 0, 0))],
        out_specs=(pl.BlockSpec((1, ATT_TILE, n_keys), lambda h: (h, 0, 0)),
                   pl.BlockSpec((1, n_keys, ATT_TILE), lambda h: (h, 0, 0))),
        out_shape=(jax.ShapeDtypeStruct((H_B, ATT_TILE, n_keys), F32),
                   jax.ShapeDtypeStruct((H_B, n_keys, ATT_TILE), F32)),
        compiler_params=_compiler_params(("arbitrary",)),
        name="bias_table",
    )(g)


def _proj_kernel(x_ref, tab_ref, ga_ref, gq_ref, gkv_ref, w1_ref, wq2_ref, wkv2_ref,
                 qn_ref, qr_ref, kn_ref, krt_ref, v_ref, qb_ref, kb_ref, vb_ref,
                 ckv_ref, kr_ref, bk_ref, bv_ref, *, tiles_per_seq, keep_from_tile, seq_minor):
    xn = _rms(x_ref[...], ga_ref[...]).astype(BF16)
    cos = tab_ref[:, 0:LANES]
    sin = tab_ref[:, LANES:2 * LANES]

    c_q = _dot(xn, w1_ref[:, 0:Q_RANK])
    cqn = _rms(c_q, gq_ref[...]).astype(BF16)
    q2 = _dot(cqn, wq2_ref[...])
    n_nope = H_A * NOPE_DIM
    n_rope = H_A * ROPE_DIM
    qn_ref[...] = (q2[:, 0:n_nope] * (MLA_SCALE * LOG2E)).astype(BF16)
    cos2 = jnp.concatenate([cos] * (n_rope // LANES), axis=1)
    sin2 = jnp.concatenate([sin] * (n_rope // LANES), axis=1)
    q_rope = q2[:, n_nope:n_nope + n_rope] * cos2 + q2[:, n_nope + n_rope:] * sin2
    qr_ref[...] = (q_rope * (MLA_SCALE * LOG2E)).astype(BF16)

    o = Q_RANK
    c_kv = _dot(xn, w1_ref[:, o:o + KV_RANK])
    ckvn = _rms(c_kv, gkv_ref[...])
    ckv_ref[...] = ckvn
    kv2 = _dot(ckvn.astype(BF16), wkv2_ref[...])
    kn_ref[...] = kv2[:, 0:n_nope].astype(BF16)
    v_ref[...] = kv2[:, n_nope:].astype(BF16)

    o += KV_RANK
    krr = _dot(xn, w1_ref[:, o:o + 2 * LANES])
    k_rope = krr[:, 0:LANES] * cos + krr[:, LANES:] * sin
    krt_ref[...] = k_rope.astype(BF16)
    if seq_minor:
        kr_ref[0] = k_rope.T[0:ROPE_DIM, :]
    else:
        kr_ref[...] = k_rope[:, 0:ROPE_DIM]

    o += 2 * LANES
    qkv = _dot(xn, w1_ref[:, o:o + 3 * MIX_B])
    qb_ref[...] = (qkv[:, 0:MIX_B] * (BAND_SCALE * LOG2E)).astype(BF16)
    kb = qkv[:, MIX_B:2 * MIX_B]
    vb = qkv[:, 2 * MIX_B:]
    kb_ref[...] = kb.astype(BF16)
    vb_ref[...] = vb.astype(BF16)

    @pl.when(pl.program_id(0) % tiles_per_seq >= keep_from_tile)
    def _():
        bk_ref[0] = kb.T if seq_minor else kb
        bv_ref[0] = vb.T if seq_minor else vb


def _project(x2d, tab, weights, *, seq_len, n_keep, tm, seq_minor):
    n, d = x2d.shape
    g_attn, g_q, g_kv, w1, wq2, wkv2 = weights
    assert n % seq_len == 0 and seq_len % tm == 0 and n_keep % tm == 0
    tiles_per_seq = seq_len // tm
    keep_from_tile = (seq_len - n_keep) // tm
    n_seq = n // seq_len
    keep_tiles = n_keep // tm

    row = lambda width: pl.BlockSpec((tm, width), lambda i: (i, 0))
    keep_tile = lambda i: jnp.maximum(i % tiles_per_seq - keep_from_tile, 0)
    if seq_minor:
        kr_spec = pl.BlockSpec((1, ROPE_DIM, tm), lambda i: (i // tiles_per_seq, 0, i % tiles_per_seq))
        kr_shape = jax.ShapeDtypeStruct((n_seq, ROPE_DIM, seq_len), F32)
        keep_spec = pl.BlockSpec((1, MIX_B, tm), lambda i: (i // tiles_per_seq, 0, keep_tile(i)))
        keep_shape = jax.ShapeDtypeStruct((n_seq, MIX_B, keep_tiles * tm), F32)
    else:
        kr_spec = row(ROPE_DIM)
        kr_shape = jax.ShapeDtypeStruct((n, ROPE_DIM), F32)
        keep_spec = pl.BlockSpec((1, tm, MIX_B), lambda i: (i // tiles_per_seq, keep_tile(i), 0))
        keep_shape = jax.ShapeDtypeStruct((n_seq, keep_tiles * tm, MIX_B), F32)
    bf = lambda width: jax.ShapeDtypeStruct((n, width), BF16)
    out_shape = (
        bf(H_A * NOPE_DIM), bf(H_A * ROPE_DIM), bf(H_A * NOPE_DIM), bf(LANES), bf(MIX_A),
        bf(MIX_B), bf(MIX_B), bf(MIX_B),
        jax.ShapeDtypeStruct((n, KV_RANK), F32), kr_shape, keep_shape, keep_shape,
    )
    out_specs = (
        row(H_A * NOPE_DIM), row(H_A * ROPE_DIM), row(H_A * NOPE_DIM), row(LANES), row(MIX_A),
        row(MIX_B), row(MIX_B), row(MIX_B), row(KV_RANK), kr_spec, keep_spec, keep_spec,
    )
    in_specs = [
        row(d),
        pl.BlockSpec((tm, 2 * LANES), lambda i: (i % tiles_per_seq, 0)),
        _const_spec(g_attn.shape), _const_spec(g_q.shape), _const_spec(g_kv.shape),
        _const_spec(w1.shape), _const_spec(wq2.shape), _const_spec(wkv2.shape),
    ]
    return pl.pallas_call(
        functools.partial(_proj_kernel, tiles_per_seq=tiles_per_seq, keep_from_tile=keep_from_tile,
                          seq_minor=seq_minor),
        grid=(n // tm,),
        in_specs=in_specs,
        out_specs=out_specs,
        out_shape=out_shape,
        compiler_params=_compiler_params(("arbitrary",)),
        name="projection",
    )(x2d, tab, g_attn, g_q, g_kv, w1, wq2, wkv2)


def _store_value_rows(v_ref, ve_ref, vo_ref):
    vt = v_ref[...].astype(F32).T
    own_even = lax.broadcasted_iota(jnp.int32, (LANES, 1), 0) < V_DIM
    ve_ref[...] = jnp.where(own_even, vt, 1.0).astype(BF16)
    vo_ref[...] = jnp.where(own_even, 1.0, vt).astype(BF16)


def _pipelined_softmax(tiles, scores, attend):
    s_next = scores(tiles[0])
    pending = None
    for i, tile in enumerate(tiles):
        s = s_next
        if i + 1 < len(tiles):
            s_next = scores(tiles[i + 1])
        p = jnp.exp2(s - jnp.max(s, axis=0, keepdims=True)).astype(BF16)
        if pending is not None:
            attend(*pending)
        pending = (tile, p)
    attend(*pending)


def _normalised_pair_t(acc_even, acc_odd):
    own_even = lax.broadcasted_iota(jnp.int32, (LANES, 1), 0) < V_DIM
    num = jnp.where(own_even, acc_even, acc_odd)
    sums = jnp.where(own_even, acc_odd, acc_even)
    den = jnp.concatenate([sums[V_DIM:], sums[:V_DIM]], axis=0)
    return (num / den).T


def _mla_kernel(qn_ref, qr_ref, kn_ref, krt_ref, v_ref, o_ref, ve_ref, vo_ref, *, n_tiles):
    t = ATT_TILE
    lane = _lane_iota()
    low = lane < V_DIM
    _store_value_rows(v_ref, ve_ref, vo_ref)
    rope_base = (pl.program_id(1) % (ROPE_PER_BLOCK // 2)) * 2 * ROPE_DIM
    key_chunk = lax.broadcasted_iota(jnp.int32, (t, 2 * t), 0) // CHUNK
    query_chunk = (lax.broadcasted_iota(jnp.int32, (t, 2 * t), 1) % t) // CHUNK
    causal = key_chunk <= query_chunk

    def scores(qi):
        r0, n_keys = qi * t, (qi + 1) * t
        qn = qn_ref[r0:r0 + t, :]
        qr = qr_ref[r0:r0 + t, :]
        k2 = jnp.concatenate([kn_ref[0:n_keys, :], krt_ref[0:n_keys, :]], axis=1)
        q_heads = []
        for hh in range(2):
            nope_mask = low if hh == 0 else jnp.logical_not(low)
            lo = rope_base + hh * ROPE_DIM
            rope_mask = jnp.logical_and(lane >= lo, lane < lo + ROPE_DIM)
            q_heads.append(jnp.concatenate(
                [jnp.where(nope_mask, qn, jnp.zeros_like(qn)),
                 jnp.where(rope_mask, qr, jnp.zeros_like(qr))], axis=1))
        s = _dot_nt(k2, jnp.concatenate(q_heads, axis=0))
        s_diag = jnp.where(causal, s[r0:, :], NEG_INF)
        return s_diag if qi == 0 else jnp.concatenate([s[:r0, :], s_diag], axis=0)

    def attend(qi, p):
        r0, n_keys = qi * t, (qi + 1) * t
        o_ref[r0:r0 + t, :] = _normalised_pair_t(
            _dot(ve_ref[:, 0:n_keys], p[:, :t]), _dot(vo_ref[:, 0:n_keys], p[:, t:]))

    _pipelined_softmax(list(range(n_tiles)), scores, attend)


def _mla_prompt(qn, qr, kn, krt, v, *, seq_len):
    n = qn.shape[0]
    assert seq_len % ATT_TILE == 0
    n_pairs = H_A // 2
    blk = lambda f: pl.BlockSpec((seq_len, LANES), f)
    return pl.pallas_call(
        functools.partial(_mla_kernel, n_tiles=seq_len // ATT_TILE),
        grid=(n // seq_len, n_pairs),
        in_specs=[
            blk(lambda b, j: (b, j)),
            blk(lambda b, j: (b, j // (ROPE_PER_BLOCK // 2))),
            blk(lambda b, j: (b, j)),
            blk(lambda b, j: (b, 0)),
            blk(lambda b, j: (b, j)),
        ],
        out_specs=blk(lambda b, j: (b, j)),
        out_shape=jax.ShapeDtypeStruct((n, MIX_A), F32),
        scratch_shapes=[pltpu.VMEM((LANES, seq_len), BF16), pltpu.VMEM((LANES, seq_len), BF16)],
        compiler_params=_compiler_params(("arbitrary", "arbitrary")),
        name="mla_prompt",
    )(qn, qr, kn, krt, v)


def _band_kernel(q_ref, k_ref, v_ref, t_ref, o_ref, ve_ref, vo_ref, *, n_tiles):
    t = ATT_TILE
    low = _lane_iota() < HD_B
    _store_value_rows(v_ref, ve_ref, vo_ref)

    def window(g):
        return max(g - (BAND_TILES - 1), 0) * t, (g + 1) * t

    def scores(g):
        k_lo, k_hi = window(g)
        q = q_ref[g * t:(g + 1) * t, :]
        zero_q = jnp.zeros_like(q)
        q2 = jnp.concatenate([jnp.where(low, q, zero_q), jnp.where(low, zero_q, q)], axis=0)
        first = BAND_TILES * t - (k_hi - k_lo)
        bias = jnp.concatenate([t_ref[0, first:, :], t_ref[1, first:, :]], axis=1)
        return _dot_nt(k_ref[k_lo:k_hi, :], q2) + bias

    def attend(g, p):
        k_lo, k_hi = window(g)
        o_ref[g * t:(g + 1) * t, :] = _normalised_pair_t(
            _dot(ve_ref[:, k_lo:k_hi], p[:, :t]), _dot(vo_ref[:, k_lo:k_hi], p[:, t:]))

    _pipelined_softmax(list(range(n_tiles)), scores, attend)


def _band_prompt(qb, kb, vb, table, *, seq_len):
    n = qb.shape[0]
    assert seq_len % ATT_TILE == 0
    blk = pl.BlockSpec((seq_len, LANES), lambda b, j: (b, j))
    return pl.pallas_call(
        functools.partial(_band_kernel, n_tiles=seq_len // ATT_TILE),
        grid=(n // seq_len, H_B // 2),
        in_specs=[blk, blk, blk,
                  pl.BlockSpec((2,) + table.shape[1:], lambda b, j: (j, 0, 0))],
        out_specs=blk,
        out_shape=jax.ShapeDtypeStruct((n, MIX_B), F32),
        scratch_shapes=[pltpu.VMEM((LANES, seq_len), BF16), pltpu.VMEM((LANES, seq_len), BF16)],
        compiler_params=_compiler_params(("arbitrary", "arbitrary")),
        name="band_prompt",
    )(qb, kb, vb, table)


def _sample_kernel(qn_ref, qr_ref, ckvn_ref, krtn_ref, qb_ref, kbn_ref, vbn_ref,
                   cckv_ref, ckrt_ref, cbkt_ref, cbvt_ref, t_ref, wukt_ref, wuv_ref,
                   oa_ref, ob_ref, *, n_tok):
    lane = _lane_iota()
    low = lane < V_DIM

    qn = qn_ref[...]
    qr = qr_ref[...]
    q_lat, q_rope = [], []
    for h in range(H_A):
        pair, par = divmod(h, 2)
        blk = qn[:, pair * LANES:(pair + 1) * LANES]
        head_mask = low if par == 0 else jnp.logical_not(low)
        q_h = jnp.where(head_mask, blk, jnp.zeros_like(blk))
        q_lat.append(_dot(q_h, wukt_ref[pair * LANES:(pair + 1) * LANES, :]))
        rblk, rpos = divmod(h, ROPE_PER_BLOCK)
        r = qr[:, rblk * LANES:(rblk + 1) * LANES]
        own = jnp.logical_and(lane >= rpos * ROPE_DIM, lane < (rpos + 1) * ROPE_DIM)
        q_rope.append(jnp.where(own, r, jnp.zeros_like(r)))
    q_lat = jnp.concatenate(q_lat, axis=0).astype(BF16)
    q_rope = jnp.concatenate(q_rope, axis=0)

    ckv_c = cckv_ref[0].astype(BF16)
    krt_c = jnp.concatenate([ckrt_ref[0].astype(BF16)] * ROPE_PER_BLOCK, axis=0)
    ckv_n = ckvn_ref[...].astype(BF16)
    s_c = _dot_nt(q_lat, ckv_c) + _dot(q_rope, krt_c)
    s_n = _dot_nt(q_lat, ckv_n) + _dot_nt(q_rope, krtn_ref[...])
    m = jnp.maximum(jnp.max(s_c, axis=1, keepdims=True), jnp.max(s_n, axis=1, keepdims=True))
    p_c = jnp.exp2(s_c - m)
    p_n = jnp.exp2(s_n - m)
    l = jnp.sum(p_c, axis=1, keepdims=True) + jnp.sum(p_n, axis=1, keepdims=True)
    o_lat = ((_dot(p_c.astype(BF16), ckv_c) + _dot(p_n.astype(BF16), ckv_n)) / l).astype(BF16)
    wuv = wuv_ref[...]
    col_head = lax.broadcasted_iota(jnp.int32, (1, MIX_A), 1) // V_DIM
    out_a = None
    for h in range(H_A):
        w_h = jnp.where(col_head == h, wuv, jnp.zeros_like(wuv))
        o_h = _dot(o_lat[h * n_tok:(h + 1) * n_tok], w_h)
        out_a = o_h if out_a is None else out_a + o_h
    oa_ref[...] = out_a

    n_cache = cbkt_ref.shape[2]
    qb = qb_ref[...]
    kb_n = kbn_ref[...]
    vb_n = vbn_ref[...]
    band_off = BAND_PAST - n_cache
    out_pairs = []
    for pair in range(H_B // 2):
        cols = slice(pair * LANES, (pair + 1) * LANES)
        blk = qb[:, cols]
        q2 = jnp.concatenate([jnp.where(low, blk, jnp.zeros_like(blk)),
                              jnp.where(low, jnp.zeros_like(blk), blk)], axis=0)
        bias = jnp.concatenate([t_ref[2 * pair], t_ref[2 * pair + 1]], axis=0)
        kt_c = cbkt_ref[0, cols, :].astype(BF16)
        vt_c = cbvt_ref[0, cols, :].astype(BF16)
        s_c = _dot(q2, kt_c) + bias[:, band_off:BAND_PAST]
        s_n = _dot_nt(q2, kb_n[:, cols]) + bias[:, BAND_PAST:BAND_PAST + n_tok]
        m = jnp.maximum(jnp.max(s_c, axis=1, keepdims=True), jnp.max(s_n, axis=1, keepdims=True))
        p_c = jnp.exp2(s_c - m)
        p_n = jnp.exp2(s_n - m)
        l = jnp.sum(p_c, axis=1, keepdims=True) + jnp.sum(p_n, axis=1, keepdims=True)
        o2 = (_dot_nt(p_c.astype(BF16), vt_c) + _dot(p_n.astype(BF16), vb_n[:, cols])) / l
        out_pairs.append(jnp.where(low, o2[0:n_tok], o2[n_tok:2 * n_tok]))
    ob_ref[...] = jnp.concatenate(out_pairs, axis=1)


def _sample_attention(proj, caches, table, wukt, wuv, *, n_streams, n_tok):
    qn, qr, _, krt, _, qb, kb, vb, ckv, _, _, _ = proj
    cache_ckv, cache_kr, cache_bk, cache_bv = caches
    n_band = cache_bk.shape[1]
    assert n_band <= BAND_PAST and BAND_PAST + n_tok <= table.shape[2]
    ckrt = jnp.swapaxes(cache_kr, 1, 2)
    cbkt = jnp.transpose(cache_bk, (0, 2, 3, 1)).reshape(n_streams, MIX_B, n_band)
    cbvt = jnp.transpose(cache_bv, (0, 2, 3, 1)).reshape(n_streams, MIX_B, n_band)
    tok = lambda width: pl.BlockSpec((n_tok, width), lambda b: (b, 0))
    per_stream = lambda a: pl.BlockSpec((1,) + a.shape[1:], lambda b: (b, 0, 0))
    out = jax.ShapeDtypeStruct((n_streams * n_tok, MIX_A), F32)
    return pl.pallas_call(
        functools.partial(_sample_kernel, n_tok=n_tok),
        grid=(n_streams,),
        in_specs=[
            tok(qn.shape[1]), tok(qr.shape[1]), tok(ckv.shape[1]), tok(krt.shape[1]),
            tok(qb.shape[1]), tok(kb.shape[1]), tok(vb.shape[1]),
            per_stream(cache_ckv), per_stream(ckrt), per_stream(cbkt), per_stream(cbvt),
            pl.BlockSpec((H_B, n_tok, table.shape[2]), lambda b: (0, 0, 0)),
            _const_spec(wukt.shape), _const_spec(wuv.shape),
        ],
        out_specs=(tok(MIX_A), tok(MIX_B)),
        out_shape=(out, out),
        compiler_params=_compiler_params(("arbitrary",)),
        name="sample_attention",
    )(qn, qr, ckv, krt, qb, kb, vb, cache_ckv, ckrt, cbkt, cbvt, table, wukt, wuv)


def _ffn_chunks(d_ff):
    chunks, start = [], 0
    while start < d_ff:
        size = min(2 * MXU_DIM, d_ff - start)
        chunks.append((start, size))
        start += size
    return chunks


def _out_kernel(x_ref, oa_ref, ob_ref, goa_ref, gob_ref, gffn_ref, gfin_ref,
                wout_ref, wg_ref, wu_ref, wd_ref, y_ref, *, sub_rows):
    chunks = _ffn_chunks(wg_ref.shape[1])

    def mixed(rows):
        mix = jnp.concatenate([_rms(oa_ref[rows, :], goa_ref[...]), _rms(ob_ref[rows, :], gob_ref[...])],
                              axis=1).astype(BF16)
        x1 = x_ref[rows, :] + _dot(mix, wout_ref[...])
        return x1, _rms(x1, gffn_ref[...]).astype(BF16)

    def ffn_part(h, chunk):
        start, size = chunk
        gate = _dot(h, wg_ref[:, start:start + size])
        up = _dot(h, wu_ref[:, start:start + size])
        act = (jax.nn.silu(gate) * up).astype(BF16)
        return _dot(act, wd_ref[start:start + size, :])

    n_sub = x_ref.shape[0] // sub_rows
    tiles = [slice(r * sub_rows, (r + 1) * sub_rows) for r in range(n_sub)]
    state = mixed(tiles[0])
    for r, rows in enumerate(tiles):
        x1, h = state
        ffn = ffn_part(h, chunks[0])
        if r + 1 < n_sub:
            state = mixed(tiles[r + 1])
        for chunk in chunks[1:]:
            ffn = ffn + ffn_part(h, chunk)
        y_ref[rows, :] = _rms(x1 + ffn, gfin_ref[...])


def _output(x2d, oa, ob, weights, *, tm, sub_rows):
    n, d = x2d.shape
    assert n % tm == 0 and tm % sub_rows == 0
    row = lambda width: pl.BlockSpec((tm, width), lambda i: (i, 0))
    return pl.pallas_call(
        functools.partial(_out_kernel, sub_rows=sub_rows),
        grid=(n // tm,),
        in_specs=[row(d), row(MIX_A), row(MIX_B)] + [_const_spec(w.shape) for w in weights],
        out_specs=row(d),
        out_shape=jax.ShapeDtypeStruct((n, d), F32),
        compiler_params=_compiler_params(("arbitrary",)),
        name="output_ffn",
    )(x2d, oa, ob, *weights)


def _rotate_half_cols(w):
    half = ROPE_DIM // 2
    return jnp.concatenate([-w[..., half:], w[..., :half]], axis=-1)


def _layout_weights(w_in, w_uq, w_uk, w_uv):
    d = w_in.shape[0]
    o = 0
    w_cq = w_in[:, o:o + Q_RANK]; o += Q_RANK
    w_ckv = w_in[:, o:o + KV_RANK]; o += KV_RANK
    w_kr = w_in[:, o:o + ROPE_DIM]; o += ROPE_DIM
    w_qb = w_in[:, o:o + MIX_B]; o += MIX_B
    w_kvb = w_in[:, o:]
    w1 = jnp.concatenate(
        [w_cq, w_ckv, jnp.tile(w_kr, (1, ROPE_PER_BLOCK)),
         jnp.tile(_rotate_half_cols(w_kr), (1, ROPE_PER_BLOCK)),
         w_qb, w_kvb], axis=1).astype(BF16)
    uq = w_uq.reshape(Q_RANK, H_A, NOPE_DIM + ROPE_DIM)
    uq_rope = uq[:, :, NOPE_DIM:]
    wq2 = jnp.concatenate(
        [uq[:, :, :NOPE_DIM].reshape(Q_RANK, H_A * NOPE_DIM),
         uq_rope.reshape(Q_RANK, H_A * ROPE_DIM),
         _rotate_half_cols(uq_rope).reshape(Q_RANK, H_A * ROPE_DIM)], axis=1).astype(BF16)
    uk = w_uk.reshape(KV_RANK, H_A * NOPE_DIM)
    uv = w_uv.reshape(KV_RANK, H_A * V_DIM)
    wkv2 = jnp.concatenate([uk, uv], axis=1).astype(BF16)
    return w1, wq2, wkv2, uk.T.astype(BF16), uv.astype(BF16)


def _rope_table(pos):
    inv = ROPE_THETA ** (-jnp.arange(0, ROPE_DIM, 2, dtype=F32) / ROPE_DIM)
    ang = pos.astype(F32)[:, None] * inv[None, :]
    cos = jnp.tile(jnp.cos(ang), (1, 2 * ROPE_PER_BLOCK))
    sin = jnp.tile(jnp.sin(ang), (1, 2 * ROPE_PER_BLOCK))
    return jnp.concatenate([cos, sin], axis=1)


def _row_tile(n, cap):
    tm = min(n, cap)
    assert n % tm == 0
    return tm


def kernel(x_prompt, x_sample, cache_mla_ckv, cache_mla_krope, cache_band_k, cache_band_v,
           w_in, g_attn, g_q, w_uq, g_kv, w_uk, w_uv, rel_bias, g_out_a, g_out_b, w_out,
           g_ffn, w_gate, w_up, w_down, g_final):
    depth = w_in.shape[0]
    assert depth == 1, "single-layer trunk"
    batch, seq, d = x_prompt.shape
    n_streams, n_tok, _ = x_sample.shape
    past = cache_mla_ckv.shape[2]

    w1, wq2, wkv2, wukt, wuv = _layout_weights(w_in[0], w_uq[0], w_uk[0], w_uv[0])
    proj_w = (g_attn, g_q, g_kv, w1, wq2, wkv2)
    out_w = (g_out_a, g_out_b, g_ffn, g_final[None, :], w_out[0].astype(BF16),
             w_gate[0].astype(BF16), w_up[0].astype(BF16), w_down[0].astype(BF16))
    table, table_k = _bias_table(rel_bias[0])

    n_keep = min(BAND_PAST, seq)
    xp = x_prompt.reshape(batch * seq, d)
    tm = _row_tile(n_keep, 512)
    proj = _project(xp, _rope_table(jnp.arange(seq, dtype=jnp.int32)), proj_w,
                    seq_len=seq, n_keep=n_keep, tm=tm, seq_minor=True)
    qn, qr, kn, krt, v, qb, kb, vb, ckv_p, krt_p, bkt_p, bvt_p = proj
    kr_p = jnp.swapaxes(krt_p, 1, 2)
    band_state = lambda s: jnp.transpose(s.reshape(batch, H_B, HD_B, n_keep), (0, 3, 1, 2))
    bk_p, bv_p = band_state(bkt_p), band_state(bvt_p)
    oa = _mla_prompt(qn, qr, kn, krt, v, seq_len=seq)
    ob = _band_prompt(qb, kb, vb, table_k, seq_len=seq)
    out_tm = 2 * tm if (batch * seq) % (2 * tm) == 0 else tm
    y_prompt = _output(xp, oa, ob, out_w, tm=out_tm, sub_rows=tm).reshape(batch, seq, d)

    n_s = n_streams * n_tok
    xs = x_sample.reshape(n_s, d)
    pos_s = past + jnp.tile(jnp.arange(n_tok, dtype=jnp.int32), n_streams)
    proj_s = _project(xs, _rope_table(pos_s), proj_w, seq_len=n_s, n_keep=n_s, tm=n_s,
                      seq_minor=False)
    oa_s, ob_s = _sample_attention(
        proj_s, (cache_mla_ckv[0], cache_mla_krope[0], cache_band_k[0], cache_band_v[0]),
        table, wukt, wuv, n_streams=n_streams, n_tok=n_tok)
    y_sample = _output(xs, oa_s, ob_s, out_w, tm=n_s, sub_rows=n_s).reshape(n_streams, n_tok, d)
    ckv_s, kr_s, bk_s, bv_s = proj_s[8:12]

    return (
        y_prompt, y_sample,
        ckv_p.reshape(1, batch, seq, KV_RANK), kr_p.reshape(1, batch, seq, ROPE_DIM),
        bk_p.reshape(1, batch, n_keep, H_B, HD_B), bv_p.reshape(1, batch, n_keep, H_B, HD_B),
        ckv_s.reshape(1, n_streams, n_tok, KV_RANK), kr_s.reshape(1, n_streams, n_tok, ROPE_DIM),
        bk_s.reshape(1, n_streams, n_tok, H_B, HD_B), bv_s.reshape(1, n_streams, n_tok, H_B, HD_B),
    )
```

```python
import functools

import jax
import jax.numpy as jnp
from jax import lax
from jax.experimental import pallas as pl
from jax.experimental.pallas import tpu as pltpu

CHUNK = 64
EPS = 1e-6
NEG_INF = -1e30
H_A = 8
NOPE_DIM = 64
ROPE_DIM = 32
V_DIM = 64
Q_RANK = 256
KV_RANK = 256
ROPE_THETA = 10000.0
MLA_SCALE = (NOPE_DIM + ROPE_DIM) ** -0.5
H_B = 8
HD_B = 64
N_PREV_CHUNKS = 8
BAND_PAST = N_PREV_CHUNKS * CHUNK
REL_CLIP = 256
BAND_SCALE = HD_B ** -0.5
LOG2E = 1.4426950408889634
MIX_A = H_A * V_DIM
MIX_B = H_B * HD_B

LANES = 128
MXU_DIM = 256
VMEM_LIMIT_BYTES = 56 * 1024 * 1024

ATT_TILE = 4 * CHUNK
BAND_TILES = N_PREV_CHUNKS * CHUNK // ATT_TILE + 1
BAND_Q = 2 * CHUNK
BAND_KEYS = N_PREV_CHUNKS * CHUNK + BAND_Q
PAIR = 2 * V_DIM
ROPE_PER_BLOCK = LANES // ROPE_DIM

F32 = jnp.float32
BF16 = jnp.bfloat16


def _dot(a, b):
    return jnp.dot(a, b, preferred_element_type=F32)


def _dot_nt(a, b):
    return lax.dot_general(a, b, (((1,), (1,)), ((), ())), preferred_element_type=F32)


def _rms(x, g):
    return x * lax.rsqrt(jnp.mean(x * x, axis=-1, keepdims=True) + EPS) * g


def _lane_iota(width=LANES):
    return lax.broadcasted_iota(jnp.int32, (1, width), 1)


def _compiler_params(semantics):
    return pltpu.CompilerParams(dimension_semantics=semantics, vmem_limit_bytes=VMEM_LIMIT_BYTES)


def _const_spec(shape):
    nd = len(shape)
    return pl.BlockSpec(shape, lambda *_: (0,) * nd, pipeline_mode=pl.Buffered(1))


def _bias_kernel(g_ref, t_ref, tk_ref):
    rows, width = t_ref.shape[1], g_ref.shape[2]
    x = jnp.broadcast_to(g_ref[0], (rows, width))
    x = pltpu.roll(x, 0, 1, stride=1, stride_axis=0)
    x = x[:, width - t_ref.shape[2]:]
    qc = lax.broadcasted_iota(jnp.int32, x.shape, 0) // CHUNK
    kc = lax.broadcasted_iota(jnp.int32, x.shape, 1) // CHUNK
    visible = jnp.logical_and(kc >= qc, kc <= qc + N_PREV_CHUNKS)
    table = jnp.where(visible, x * LOG2E, NEG_INF)
    t_ref[0] = table
    tk_ref[0] = table.T


def _bias_table(rel_bias):
    n_keys = BAND_TILES * ATT_TILE
    width = n_keys + ATT_TILE
    n_const = width - 2 * REL_CLIP + 1
    g = jnp.concatenate(
        [jnp.broadcast_to(rel_bias[:, 2 * REL_CLIP:], (H_B, n_const)),
         rel_bias[:, 2 * REL_CLIP - 1:0:-1]], axis=1)
    g = g.reshape(H_B, 1, width)
    return pl.pallas_call(
        _bias_kernel,
        grid=(H_B,),
        in_specs=[pl.BlockSpec((1, 1, width), lambda h: (h, 0, 0))],
        out_specs=(pl.BlockSpec((1, ATT_TILE, n_keys), lambda h: (h, 0, 0)),
                   pl.BlockSpec((1, n_keys, ATT_TILE), lambda h: (h, 0, 0))),
        out_shape=(jax.ShapeDtypeStruct((H_B, ATT_TILE, n_keys), F32),
                   jax.ShapeDtypeStruct((H_B, n_keys, ATT_TILE), F32)),
        compiler_params=_compiler_params(("arbitrary",)),
        name="bias_table",
    )(g)


def _proj_kernel(x_ref, tab_ref, ga_ref, gq_ref, gkv_ref, w1_ref, wq2_ref, wkv2_ref,
                 qn_ref, qr_ref, kn_ref, krt_ref, v_ref, qb_ref, kb_ref, vb_ref,
                 ckv_ref, kr_ref, bk_ref, bv_ref, *, tiles_per_seq, keep_from_tile, seq_minor):
    xn = _rms(x_ref[...], ga_ref[...]).astype(BF16)
    cos = tab_ref[:, 0:LANES]
    sin = tab_ref[:, LANES:2 * LANES]

    n_nope = H_A * NOPE_DIM
    n_rope = H_A * ROPE_DIM
    o_kv = Q_RANK
    o_kr = o_kv + KV_RANK
    o_b = o_kr + 2 * LANES

    c_q = _dot(xn, w1_ref[:, 0:Q_RANK])
    c_kv = _dot(xn, w1_ref[:, o_kv:o_kv + KV_RANK])
    krr = _dot(xn, w1_ref[:, o_kr:o_kr + 2 * LANES])
    qkv = _dot(xn, w1_ref[:, o_b:o_b + 3 * MIX_B])

    qb_ref[...] = (qkv[:, 0:MIX_B] * (BAND_SCALE * LOG2E)).astype(BF16)
    kb = qkv[:, MIX_B:2 * MIX_B]
    vb = qkv[:, 2 * MIX_B:]
    kb_ref[...] = kb.astype(BF16)
    vb_ref[...] = vb.astype(BF16)

    cqn = _rms(c_q, gq_ref[...]).astype(BF16)
    q2 = _dot(cqn, wq2_ref[...])
    qn_ref[...] = (q2[:, 0:n_nope] * (MLA_SCALE * LOG2E)).astype(BF16)
    cos2 = jnp.concatenate([cos] * (n_rope // LANES), axis=1)
    sin2 = jnp.concatenate([sin] * (n_rope // LANES), axis=1)
    q_rope = q2[:, n_nope:n_nope + n_rope] * cos2 + q2[:, n_nope + n_rope:] * sin2
    qr_ref[...] = (q_rope * (MLA_SCALE * LOG2E)).astype(BF16)

    ckvn = _rms(c_kv, gkv_ref[...])
    ckv_ref[...] = ckvn
    kv2 = _dot(ckvn.astype(BF16), wkv2_ref[...])
    kn_ref[...] = kv2[:, 0:n_nope].astype(BF16)
    v_ref[...] = kv2[:, n_nope:].astype(BF16)

    k_rope = krr[:, 0:LANES] * cos + krr[:, LANES:] * sin
    krt_ref[...] = k_rope.astype(BF16)
    if seq_minor:
        kr_ref[0] = k_rope.T[0:ROPE_DIM, :]
    else:
        kr_ref[...] = k_rope[:, 0:ROPE_DIM]

    @pl.when(pl.program_id(0) % tiles_per_seq >= keep_from_tile)
    def _():
        bk_ref[0] = kb.T if seq_minor else kb
        bv_ref[0] = vb.T if seq_minor else vb


def _project(x2d, tab, weights, *, seq_len, n_keep, tm, seq_minor):
    n, d = x2d.shape
    g_attn, g_q, g_kv, w1, wq2, wkv2 = weights
    assert n % seq_len == 0 and seq_len % tm == 0 and n_keep % tm == 0
    tiles_per_seq = seq_len // tm
    keep_from_tile = (seq_len - n_keep) // tm
    n_seq = n // seq_len
    keep_tiles = n_keep // tm

    row = lambda width: pl.BlockSpec((tm, width), lambda i: (i, 0))
    keep_tile = lambda i: jnp.maximum(i % tiles_per_seq - keep_from_tile, 0)
    if seq_minor:
        kr_spec = pl.BlockSpec((1, ROPE_DIM, tm), lambda i: (i // tiles_per_seq, 0, i % tiles_per_seq))
        kr_shape = jax.ShapeDtypeStruct((n_seq, ROPE_DIM, seq_len), F32)
        keep_spec = pl.BlockSpec((1, MIX_B, tm), lambda i: (i // tiles_per_seq, 0, keep_tile(i)))
        keep_shape = jax.ShapeDtypeStruct((n_seq, MIX_B, keep_tiles * tm), F32)
    else:
        kr_spec = row(ROPE_DIM)
        kr_shape = jax.ShapeDtypeStruct((n, ROPE_DIM), F32)
        keep_spec = pl.BlockSpec((1, tm, MIX_B), lambda i: (i // tiles_per_seq, keep_tile(i), 0))
        keep_shape = jax.ShapeDtypeStruct((n_seq, keep_tiles * tm, MIX_B), F32)
    bf = lambda width: jax.ShapeDtypeStruct((n, width), BF16)
    out_shape = (
        bf(H_A * NOPE_DIM), bf(H_A * ROPE_DIM), bf(H_A * NOPE_DIM), bf(LANES), bf(MIX_A),
        bf(MIX_B), bf(MIX_B), bf(MIX_B),
        jax.ShapeDtypeStruct((n, KV_RANK), F32), kr_shape, keep_shape, keep_shape,
    )
    out_specs = (
        row(H_A * NOPE_DIM), row(H_A * ROPE_DIM), row(H_A * NOPE_DIM), row(LANES), row(MIX_A),
        row(MIX_B), row(MIX_B), row(MIX_B), row(KV_RANK), kr_spec, keep_spec, keep_spec,
    )
    in_specs = [
        row(d),
        pl.BlockSpec((tm, 2 * LANES), lambda i: (i % tiles_per_seq, 0)),
        _const_spec(g_attn.shape), _const_spec(g_q.shape), _const_spec(g_kv.shape),
        _const_spec(w1.shape), _const_spec(wq2.shape), _const_spec(wkv2.shape),
    ]
    return pl.pallas_call(
        functools.partial(_proj_kernel, tiles_per_seq=tiles_per_seq, keep_from_tile=keep_from_tile,
                          seq_minor=seq_minor),
        grid=(n // tm,),
        in_specs=in_specs,
        out_specs=out_specs,
        out_shape=out_shape,
        compiler_params=_compiler_params(("arbitrary",)),
        name="projection",
    )(x2d, tab, g_attn, g_q, g_kv, w1, wq2, wkv2)


def _store_value_rows(v_ref, ve_ref, vo_ref):
    vt = v_ref[...].astype(F32).T
    own_even = lax.broadcasted_iota(jnp.int32, (LANES, 1), 0) < V_DIM
    ve_ref[...] = jnp.where(own_even, vt, 1.0).astype(BF16)
    vo_ref[...] = jnp.where(own_even, 1.0, vt).astype(BF16)


def _pipelined_softmax(tiles, scores, attend, with_sums=False):
    s_next = scores(tiles[0])
    pending = None
    for i, tile in enumerate(tiles):
        s = s_next
        if i + 1 < len(tiles):
            s_next = scores(tiles[i + 1])
        p = jnp.exp2(s - jnp.max(s, axis=0, keepdims=True))
        sums = (jnp.sum(p, axis=0, keepdims=True),) if with_sums else ()
        if pending is not None:
            attend(*pending)
        pending = (tile, p.astype(BF16)) + sums
    attend(*pending)


def _normalised_pair_t(acc_even, acc_odd):
    own_even = lax.broadcasted_iota(jnp.int32, (LANES, 1), 0) < V_DIM
    num = jnp.where(own_even, acc_even, acc_odd)
    sums = jnp.where(own_even, acc_odd, acc_even)
    den = jnp.concatenate([sums[V_DIM:], sums[:V_DIM]], axis=0)
    return (num / den).T


def _mla_kernel(qn_ref, qr_ref, kn_ref, krt_ref, v_ref, o_ref, ve_ref, vo_ref, *, n_tiles):
    t = ATT_TILE
    lane = _lane_iota()
    low = lane < V_DIM
    _store_value_rows(v_ref, ve_ref, vo_ref)
    rope_base = (pl.program_id(1) % (ROPE_PER_BLOCK // 2)) * 2 * ROPE_DIM
    key_chunk = lax.broadcasted_iota(jnp.int32, (t, 2 * t), 0) // CHUNK
    query_chunk = (lax.broadcasted_iota(jnp.int32, (t, 2 * t), 1) % t) // CHUNK
    causal = key_chunk <= query_chunk

    def scores(qi):
        r0, n_keys = qi * t, (qi + 1) * t
        qn = qn_ref[r0:r0 + t, :]
        qr = qr_ref[r0:r0 + t, :]
        k2 = jnp.concatenate([kn_ref[0:n_keys, :], krt_ref[0:n_keys, :]], axis=1)
        q_heads = []
        for hh in range(2):
            nope_mask = low if hh == 0 else jnp.logical_not(low)
            lo = rope_base + hh * ROPE_DIM
            rope_mask = jnp.logical_and(lane >= lo, lane < lo + ROPE_DIM)
            q_heads.append(jnp.concatenate(
                [jnp.where(nope_mask, qn, jnp.zeros_like(qn)),
                 jnp.where(rope_mask, qr, jnp.zeros_like(qr))], axis=1))
        s = _dot_nt(k2, jnp.concatenate(q_heads, axis=0))
        s_diag = jnp.where(causal, s[r0:, :], NEG_INF)
        return s_diag if qi == 0 else jnp.concatenate([s[:r0, :], s_diag], axis=0)

    def attend(qi, p):
        r0, n_keys = qi * t, (qi + 1) * t
        o_ref[r0:r0 + t, :] = _normalised_pair_t(
            _dot(ve_ref[:, 0:n_keys], p[:, :t]), _dot(vo_ref[:, 0:n_keys], p[:, t:]))

    _pipelined_softmax(list(range(n_tiles)), scores, attend)


def _mla_prompt(qn, qr, kn, krt, v, *, seq_len):
    n = qn.shape[0]
    assert seq_len % ATT_TILE == 0
    n_pairs = H_A // 2
    blk = lambda f: pl.BlockSpec((seq_len, LANES), f)
    return pl.pallas_call(
        functools.partial(_mla_kernel, n_tiles=seq_len // ATT_TILE),
        grid=(n // seq_len, n_pairs),
        in_specs=[
            blk(lambda b, j: (b, j)),
            blk(lambda b, j: (b, j // (ROPE_PER_BLOCK // 2))),
            blk(lambda b, j: (b, j)),
            blk(lambda b, j: (b, 0)),
            blk(lambda b, j: (b, j)),
        ],
        out_specs=blk(lambda b, j: (b, j)),
        out_shape=jax.ShapeDtypeStruct((n, MIX_A), F32),
        scratch_shapes=[pltpu.VMEM((LANES, seq_len), BF16), pltpu.VMEM((LANES, seq_len), BF16)],
        compiler_params=_compiler_params(("arbitrary", "arbitrary")),
        name="mla_prompt",
    )(qn, qr, kn, krt, v)


def _band_kernel(q_ref, k_ref, v_ref, t_ref, o_ref, vt_ref, *, n_tiles):
    t = BAND_Q
    low = _lane_iota() < HD_B
    vt_ref[...] = v_ref[...].astype(F32).T.astype(BF16)

    def window(g):
        return max((g + 1) * t - BAND_KEYS, 0), (g + 1) * t

    def scores(g):
        k_lo, k_hi = window(g)
        q = q_ref[g * t:(g + 1) * t, :]
        zero_q = jnp.zeros_like(q)
        q2 = jnp.concatenate([jnp.where(low, q, zero_q), jnp.where(low, zero_q, q)], axis=0)
        first = BAND_KEYS - (k_hi - k_lo)
        bias = jnp.concatenate([t_ref[0, first:BAND_KEYS, 0:t], t_ref[1, first:BAND_KEYS, 0:t]], axis=1)
        return _dot_nt(k_ref[k_lo:k_hi, :], q2) + bias

    def attend(g, p, sums):
        k_lo, k_hi = window(g)
        acc = _dot(vt_ref[:, k_lo:k_hi], p)
        o_t = jnp.concatenate([acc[0:HD_B, 0:t] / sums[:, 0:t], acc[HD_B:, t:] / sums[:, t:]], axis=0)
        o_ref[g * t:(g + 1) * t, :] = o_t.T

    _pipelined_softmax(list(range(n_tiles)), scores, attend, with_sums=True)


def _band_prompt(qb, kb, vb, table, *, seq_len):
    n = qb.shape[0]
    assert seq_len % BAND_Q == 0
    blk = pl.BlockSpec((seq_len, LANES), lambda b, j: (b, j))
    return pl.pallas_call(
        functools.partial(_band_kernel, n_tiles=seq_len // BAND_Q),
        grid=(n // seq_len, H_B // 2),
        in_specs=[blk, blk, blk,
                  pl.BlockSpec((2,) + table.shape[1:], lambda b, j: (j, 0, 0))],
        out_specs=blk,
        out_shape=jax.ShapeDtypeStruct((n, MIX_B), F32),
        scratch_shapes=[pltpu.VMEM((LANES, seq_len), BF16)],
        compiler_params=_compiler_params(("arbitrary", "arbitrary")),
        name="band_prompt",
    )(qb, kb, vb, table)


def _sample_kernel(qn_ref, qr_ref, ckvn_ref, krtn_ref, qb_ref, kbn_ref, vbn_ref,
                   cckv_ref, ckrt_ref, cbkt_ref, cbvt_ref, t_ref, wukt_ref, wuv_ref,
                   oa_ref, ob_ref, *, n_tok):
    lane = _lane_iota()
    low = lane < V_DIM

    qn = qn_ref[...]
    qr = qr_ref[...]
    q_lat, q_rope = [], []
    for h in range(H_A):
        pair, par = divmod(h, 2)
        blk = qn[:, pair * LANES:(pair + 1) * LANES]
        head_mask = low if par == 0 else jnp.logical_not(low)
        q_h = jnp.where(head_mask, blk, jnp.zeros_like(blk))
        q_lat.append(_dot(q_h, wukt_ref[pair * LANES:(pair + 1) * LANES, :]))
        rblk, rpos = divmod(h, ROPE_PER_BLOCK)
        r = qr[:, rblk * LANES:(rblk + 1) * LANES]
        own = jnp.logical_and(lane >= rpos * ROPE_DIM, lane < (rpos + 1) * ROPE_DIM)
        q_rope.append(jnp.where(own, r, jnp.zeros_like(r)))
    q_lat = jnp.concatenate(q_lat, axis=0).astype(BF16)
    q_rope = jnp.concatenate(q_rope, axis=0)

    ckv_c = cckv_ref[0].astype(BF16)
    krt_c = jnp.concatenate([ckrt_ref[0].astype(BF16)] * ROPE_PER_BLOCK, axis=0)
    ckv_n = ckvn_ref[...].astype(BF16)
    s_c = _dot_nt(q_lat, ckv_c) + _dot(q_rope, krt_c)
    s_n = _dot_nt(q_lat, ckv_n) + _dot_nt(q_rope, krtn_ref[...])
    m = jnp.maximum(jnp.max(s_c, axis=1, keepdims=True), jnp.max(s_n, axis=1, keepdims=True))
    p_c = jnp.exp2(s_c - m)
    p_n = jnp.exp2(s_n - m)
    l = jnp.sum(p_c, axis=1, keepdims=True) + jnp.sum(p_n, axis=1, keepdims=True)
    o_lat = ((_dot(p_c.astype(BF16), ckv_c) + _dot(p_n.astype(BF16), ckv_n)) / l).astype(BF16)
    wuv = wuv_ref[...]
    col_head = lax.broadcasted_iota(jnp.int32, (1, MIX_A), 1) // V_DIM
    out_a = None
    for h in range(H_A):
        w_h = jnp.where(col_head == h, wuv, jnp.zeros_like(wuv))
        o_h = _dot(o_lat[h * n_tok:(h + 1) * n_tok], w_h)
        out_a = o_h if out_a is None else out_a + o_h
    oa_ref[...] = out_a

    n_cache = cbkt_ref.shape[2]
    qb = qb_ref[...]
    kb_n = kbn_ref[...]
    vb_n = vbn_ref[...]
    band_off = BAND_PAST - n_cache
    out_pairs = []
    for pair in range(H_B // 2):
        cols = slice(pair * LANES, (pair + 1) * LANES)
        blk = qb[:, cols]
        q2 = jnp.concatenate([jnp.where(low, blk, jnp.zeros_like(blk)),
                              jnp.where(low, jnp.zeros_like(blk), blk)], axis=0)
        bias = jnp.concatenate([t_ref[2 * pair], t_ref[2 * pair + 1]], axis=0)
        kt_c = cbkt_ref[0, cols, :].astype(BF16)
        vt_c = cbvt_ref[0, cols, :].astype(BF16)
        s_c = _dot(q2, kt_c) + bias[:, band_off:BAND_PAST]
        s_n = _dot_nt(q2, kb_n[:, cols]) + bias[:, BAND_PAST:BAND_PAST + n_tok]
        m = jnp.maximum(jnp.max(s_c, axis=1, keepdims=True), jnp.max(s_n, axis=1, keepdims=True))
        p_c = jnp.exp2(s_c - m)
        p_n = jnp.exp2(s_n - m)
        l = jnp.sum(p_c, axis=1, keepdims=True) + jnp.sum(p_n, axis=1, keepdims=True)
        o2 = (_dot_nt(p_c.astype(BF16), vt_c) + _dot(p_n.astype(BF16), vb_n[:, cols])) / l
        out_pairs.append(jnp.where(low, o2[0:n_tok], o2[n_tok:2 * n_tok]))
    ob_ref[...] = jnp.concatenate(out_pairs, axis=1)


def _sample_attention(proj, caches, table, wukt, wuv, *, n_streams, n_tok):
    qn, qr, _, krt, _, qb, kb, vb, ckv, _, _, _ = proj
    cache_ckv, cache_kr, cache_bk, cache_bv = caches
    n_band = cache_bk.shape[1]
    assert n_band <= BAND_PAST and BAND_PAST + n_tok <= table.shape[2]
    ckrt = jnp.swapaxes(cache_kr, 1, 2)
    cbkt = jnp.transpose(cache_bk, (0, 2, 3, 1)).reshape(n_streams, MIX_B, n_band)
    cbvt = jnp.transpose(cache_bv, (0, 2, 3, 1)).reshape(n_streams, MIX_B, n_band)
    tok = lambda width: pl.BlockSpec((n_tok, width), lambda b: (b, 0))
    per_stream = lambda a: pl.BlockSpec((1,) + a.shape[1:], lambda b: (b, 0, 0))
    out = jax.ShapeDtypeStruct((n_streams * n_tok, MIX_A), F32)
    return pl.pallas_call(
        functools.partial(_sample_kernel, n_tok=n_tok),
        grid=(n_streams,),
        in_specs=[
            tok(qn.shape[1]), tok(qr.shape[1]), tok(ckv.shape[1]), tok(krt.shape[1]),
            tok(qb.shape[1]), tok(kb.shape[1]), tok(vb.shape[1]),
            per_stream(cache_ckv), per_stream(ckrt), per_stream(cbkt), per_stream(cbvt),
            pl.BlockSpec((H_B, n_tok, table.shape[2]), lambda b: (0, 0, 0)),
            _const_spec(wukt.shape), _const_spec(wuv.shape),
        ],
        out_specs=(tok(MIX_A), tok(MIX_B)),
        out_shape=(out, out),
        compiler_params=_compiler_params(("arbitrary",)),
        name="sample_attention",
    )(qn, qr, ckv, krt, qb, kb, vb, cache_ckv, ckrt, cbkt, cbvt, table, wukt, wuv)


def _ffn_chunks(d_ff):
    chunks, start = [], 0
    while start < d_ff:
        size = min(2 * MXU_DIM, d_ff - start)
        chunks.append((start, size))
        start += size
    return chunks


def _out_kernel(x_ref, oa_ref, ob_ref, goa_ref, gob_ref, gffn_ref, gfin_ref,
                wout_ref, wg_ref, wu_ref, wd_ref, y_ref, *, sub_rows):
    chunks = _ffn_chunks(wg_ref.shape[1])

    def mixed(rows):
        mix = jnp.concatenate([_rms(oa_ref[rows, :], goa_ref[...]), _rms(ob_ref[rows, :], gob_ref[...])],
                              axis=1).astype(BF16)
        x1 = x_ref[rows, :] + _dot(mix, wout_ref[...])
        return x1, _rms(x1, gffn_ref[...]).astype(BF16)

    def ffn_part(h, chunk):
        start, size = chunk
        gate = _dot(h, wg_ref[:, start:start + size])
        up = _dot(h, wu_ref[:, start:start + size])
        act = (jax.nn.silu(gate) * up).astype(BF16)
        return _dot(act, wd_ref[start:start + size, :])

    n_sub = x_ref.shape[0] // sub_rows
    tiles = [slice(r * sub_rows, (r + 1) * sub_rows) for r in range(n_sub)]
    state = mixed(tiles[0])
    for r, rows in enumerate(tiles):
        x1, h = state
        ffn = ffn_part(h, chunks[0])
        if r + 1 < n_sub:
            state = mixed(tiles[r + 1])
        for chunk in chunks[1:]:
            ffn = ffn + ffn_part(h, chunk)
        y_ref[rows, :] = _rms(x1 + ffn, gfin_ref[...])


def _output(x2d, oa, ob, weights, *, tm, sub_rows):
    n, d = x2d.shape
    assert n % tm == 0 and tm % sub_rows == 0
    row = lambda width: pl.BlockSpec((tm, width), lambda i: (i, 0))
    return pl.pallas_call(
        functools.partial(_out_kernel, sub_rows=sub_rows),
        grid=(n // tm,),
        in_specs=[row(d), row(MIX_A), row(MIX_B)] + [_const_spec(w.shape) for w in weights],
        out_specs=row(d),
        out_shape=jax.ShapeDtypeStruct((n, d), F32),
        compiler_params=_compiler_params(("arbitrary",)),
        name="output_ffn",
    )(x2d, oa, ob, *weights)


def _rotate_half_cols(w):
    half = ROPE_DIM // 2
    return jnp.concatenate([-w[..., half:], w[..., :half]], axis=-1)


def _layout_weights(w_in, w_uq, w_uk, w_uv):
    d = w_in.shape[0]
    o = 0
    w_cq = w_in[:, o:o + Q_RANK]; o += Q_RANK
    w_ckv = w_in[:, o:o + KV_RANK]; o += KV_RANK
    w_kr = w_in[:, o:o + ROPE_DIM]; o += ROPE_DIM
    w_qb = w_in[:, o:o + MIX_B]; o += MIX_B
    w_kvb = w_in[:, o:]
    w1 = jnp.concatenate(
        [w_cq, w_ckv, jnp.tile(w_kr, (1, ROPE_PER_BLOCK)),
         jnp.tile(_rotate_half_cols(w_kr), (1, ROPE_PER_BLOCK)),
         w_qb, w_kvb], axis=1).astype(BF16)
    uq = w_uq.reshape(Q_RANK, H_A, NOPE_DIM + ROPE_DIM)
    uq_rope = uq[:, :, NOPE_DIM:]
    wq2 = jnp.concatenate(
        [uq[:, :, :NOPE_DIM].reshape(Q_RANK, H_A * NOPE_DIM),
         uq_rope.reshape(Q_RANK, H_A * ROPE_DIM),
         _rotate_half_cols(uq_rope).reshape(Q_RANK, H_A * ROPE_DIM)], axis=1).astype(BF16)
    uk = w_uk.reshape(KV_RANK, H_A * NOPE_DIM)
    uv = w_uv.reshape(KV_RANK, H_A * V_DIM)
    wkv2 = jnp.concatenate([uk, uv], axis=1).astype(BF16)
    return w1, wq2, wkv2, uk.T.astype(BF16), uv.astype(BF16)


def _rope_table(pos):
    inv = ROPE_THETA ** (-jnp.arange(0, ROPE_DIM, 2, dtype=F32) / ROPE_DIM)
    ang = pos.astype(F32)[:, None] * inv[None, :]
    cos = jnp.tile(jnp.cos(ang), (1, 2 * ROPE_PER_BLOCK))
    sin = jnp.tile(jnp.sin(ang), (1, 2 * ROPE_PER_BLOCK))
    return jnp.concatenate([cos, sin], axis=1)


def _row_tile(n, cap):
    tm = min(n, cap)
    assert n % tm == 0
    return tm


def kernel(x_prompt, x_sample, cache_mla_ckv, cache_mla_krope, cache_band_k, cache_band_v,
           w_in, g_attn, g_q, w_uq, g_kv, w_uk, w_uv, rel_bias, g_out_a, g_out_b, w_out,
           g_ffn, w_gate, w_up, w_down, g_final):
    depth = w_in.shape[0]
    assert depth == 1, "single-layer trunk"
    batch, seq, d = x_prompt.shape
    n_streams, n_tok, _ = x_sample.shape
    past = cache_mla_ckv.shape[2]

    w1, wq2, wkv2, wukt, wuv = _layout_weights(w_in[0], w_uq[0], w_uk[0], w_uv[0])
    proj_w = (g_attn, g_q, g_kv, w1, wq2, wkv2)
    out_w = (g_out_a, g_out_b, g_ffn, g_final[None, :], w_out[0].astype(BF16),
             w_gate[0].astype(BF16), w_up[0].astype(BF16), w_down[0].astype(BF16))
    table, table_k = _bias_table(rel_bias[0])

    n_keep = min(BAND_PAST, seq)
    xp = x_prompt.reshape(batch * seq, d)
    tm = _row_tile(n_keep, 512)
    proj = _project(xp, _rope_table(jnp.arange(seq, dtype=jnp.int32)), proj_w,
                    seq_len=seq, n_keep=n_keep, tm=tm, seq_minor=True)
    qn, qr, kn, krt, v, qb, kb, vb, ckv_p, krt_p, bkt_p, bvt_p = proj
    kr_p = jnp.swapaxes(krt_p, 1, 2)
    band_state = lambda s: jnp.transpose(s.reshape(batch, H_B, HD_B, n_keep), (0, 3, 1, 2))
    bk_p, bv_p = band_state(bkt_p), band_state(bvt_p)
    oa = _mla_prompt(qn, qr, kn, krt, v, seq_len=seq)
    ob = _band_prompt(qb, kb, vb, table_k, seq_len=seq)
    out_tm = 2 * tm if (batch * seq) % (2 * tm) == 0 else tm
    y_prompt = _output(xp, oa, ob, out_w, tm=out_tm, sub_rows=tm).reshape(batch, seq, d)

    n_s = n_streams * n_tok
    xs = x_sample.reshape(n_s, d)
    pos_s = past + jnp.tile(jnp.arange(n_tok, dtype=jnp.int32), n_streams)
    proj_s = _project(xs, _rope_table(pos_s), proj_w, seq_len=n_s, n_keep=n_s, tm=n_s,
                      seq_minor=False)
    oa_s, ob_s = _sample_attention(
        proj_s, (cache_mla_ckv[0], cache_mla_krope[0], cache_band_k[0], cache_band_v[0]),
        table, wukt, wuv, n_streams=n_streams, n_tok=n_tok)
    y_sample = _output(xs, oa_s, ob_s, out_w, tm=n_s, sub_rows=n_s).reshape(n_streams, n_tok, d)
    ckv_s, kr_s, bk_s, bv_s = proj_s[8:12]

    return (
        y_prompt, y_sample,
        ckv_p.reshape(1, batch, seq, KV_RANK), kr_p.reshape(1, batch, seq, ROPE_DIM),
        bk_p.reshape(1, batch, n_keep, H_B, HD_B), bv_p.reshape(1, batch, n_keep, H_B, HD_B),
        ckv_s.reshape(1, n_streams, n_tok, KV_RANK), kr_s.reshape(1, n_streams, n_tok, ROPE_DIM),
        bk_s.reshape(1, n_streams, n_tok, H_B, HD_B), bv_s.reshape(1, n_streams, n_tok, H_B, HD_B),
    )
```

```python
import functools

import jax
import jax.numpy as jnp
from jax import lax
from jax.experimental import pallas as pl
from jax.experimental.pallas import tpu as pltpu

CHUNK = 64
EPS = 1e-6
NEG_INF = -1e30
H_A = 8
NOPE_DIM = 64
ROPE_DIM = 32
V_DIM = 64
Q_RANK = 256
KV_RANK = 256
ROPE_THETA = 10000.0
MLA_SCALE = (NOPE_DIM + ROPE_DIM) ** -0.5
H_B = 8
HD_B = 64
N_PREV_CHUNKS = 8
BAND_PAST = N_PREV_CHUNKS * CHUNK
REL_CLIP = 256
BAND_SCALE = HD_B ** -0.5
LOG2E = 1.4426950408889634
MIX_A = H_A * V_DIM
MIX_B = H_B * HD_B

LANES = 128
MXU_DIM = 256
VMEM_LIMIT_BYTES = 56 * 1024 * 1024

ATT_TILE = 4 * CHUNK
BAND_TILES = N_PREV_CHUNKS * CHUNK // ATT_TILE + 1
BAND_Q = 2 * CHUNK
BAND_KEYS = N_PREV_CHUNKS * CHUNK + BAND_Q
PAIR = 2 * V_DIM
ROPE_PER_BLOCK = LANES // ROPE_DIM

F32 = jnp.float32
BF16 = jnp.bfloat16


def _dot(a, b):
    return jnp.dot(a, b, preferred_element_type=F32)


def _dot_nt(a, b):
    return lax.dot_general(a, b, (((1,), (1,)), ((), ())), preferred_element_type=F32)


def _rms(x, g):
    return x * lax.rsqrt(jnp.mean(x * x, axis=-1, keepdims=True) + EPS) * g


def _lane_iota(width=LANES):
    return lax.broadcasted_iota(jnp.int32, (1, width), 1)


def _compiler_params(semantics):
    return pltpu.CompilerParams(dimension_semantics=semantics, vmem_limit_bytes=VMEM_LIMIT_BYTES)


def _const_spec(shape):
    nd = len(shape)
    return pl.BlockSpec(shape, lambda *_: (0,) * nd, pipeline_mode=pl.Buffered(1))


def _bias_kernel(g_ref, t_ref, tk_ref):
    rows, width = t_ref.shape[1], g_ref.shape[2]
    x = jnp.broadcast_to(g_ref[0], (rows, width))
    x = pltpu.roll(x, 0, 1, stride=1, stride_axis=0)
    x = x[:, width - t_ref.shape[2]:]
    qc = lax.broadcasted_iota(jnp.int32, x.shape, 0) // CHUNK
    kc = lax.broadcasted_iota(jnp.int32, x.shape, 1) // CHUNK
    visible = jnp.logical_and(kc >= qc, kc <= qc + N_PREV_CHUNKS)
    table = jnp.where(visible, x * LOG2E, NEG_INF)
    t_ref[0] = table
    tk_ref[0] = table.T


def _bias_table(rel_bias):
    n_keys = BAND_TILES * ATT_TILE
    width = n_keys + ATT_TILE
    n_const = width - 2 * REL_CLIP + 1
    g = jnp.concatenate(
        [jnp.broadcast_to(rel_bias[:, 2 * REL_CLIP:], (H_B, n_const)),
         rel_bias[:, 2 * REL_CLIP - 1:0:-1]], axis=1)
    g = g.reshape(H_B, 1, width)
    return pl.pallas_call(
        _bias_kernel,
        grid=(H_B,),
        in_specs=[pl.BlockSpec((1, 1, width), lambda h: (h, 0, 0))],
        out_specs=(pl.BlockSpec((1, ATT_TILE, n_keys), lambda h: (h, 0, 0)),
                   pl.BlockSpec((1, n_keys, ATT_TILE), lambda h: (h, 0, 0))),
        out_shape=(jax.ShapeDtypeStruct((H_B, ATT_TILE, n_keys), F32),
                   jax.ShapeDtypeStruct((H_B, n_keys, ATT_TILE), F32)),
        compiler_params=_compiler_params(("arbitrary",)),
        name="bias_table",
    )(g)


def _proj_kernel(x_ref, tab_ref, ga_ref, gq_ref, gkv_ref, w1_ref, wq2_ref, wkv2_ref,
                 qn_ref, qr_ref, kn_ref, krt_ref, v_ref, qb_ref, kb_ref, vb_ref,
                 ckv_ref, kr_ref, bk_ref, bv_ref, *, tiles_per_seq, keep_from_tile, seq_minor):
    xn = _rms(x_ref[...], ga_ref[...]).astype(BF16)
    cos = tab_ref[:, 0:LANES]
    sin = tab_ref[:, LANES:2 * LANES]

    n_nope = H_A * NOPE_DIM
    n_rope = H_A * ROPE_DIM
    o_kv = Q_RANK
    o_kr = o_kv + KV_RANK
    o_b = o_kr + 2 * LANES

    c_q = _dot(xn, w1_ref[:, 0:Q_RANK])
    c_kv = _dot(xn, w1_ref[:, o_kv:o_kv + KV_RANK])
    krr = _dot(xn, w1_ref[:, o_kr:o_kr + 2 * LANES])
    qkv = _dot(xn, w1_ref[:, o_b:o_b + 3 * MIX_B])

    qb_ref[...] = (qkv[:, 0:MIX_B] * (BAND_SCALE * LOG2E)).astype(BF16)
    kb = qkv[:, MIX_B:2 * MIX_B]
    vb = qkv[:, 2 * MIX_B:]
    kb_ref[...] = kb.astype(BF16)
    vb_ref[...] = vb.astype(BF16)

    cqn = _rms(c_q, gq_ref[...]).astype(BF16)
    q2 = _dot(cqn, wq2_ref[...])
    qn_ref[...] = (q2[:, 0:n_nope] * (MLA_SCALE * LOG2E)).astype(BF16)
    cos2 = jnp.concatenate([cos] * (n_rope // LANES), axis=1)
    sin2 = jnp.concatenate([sin] * (n_rope // LANES), axis=1)
    q_rope = q2[:, n_nope:n_nope + n_rope] * cos2 + q2[:, n_nope + n_rope:] * sin2
    qr_ref[...] = (q_rope * (MLA_SCALE * LOG2E)).astype(BF16)

    ckvn = _rms(c_kv, gkv_ref[...])
    ckv_ref[...] = ckvn
    kv2 = _dot(ckvn.astype(BF16), wkv2_ref[...])
    kn_ref[...] = kv2[:, 0:n_nope].astype(BF16)
    v_ref[...] = kv2[:, n_nope:].astype(BF16)

    k_rope = krr[:, 0:LANES] * cos + krr[:, LANES:] * sin
    krt_ref[...] = k_rope.astype(BF16)
    if seq_minor:
        kr_ref[0] = k_rope.T[0:ROPE_DIM, :]
    else:
        kr_ref[...] = k_rope[:, 0:ROPE_DIM]

    @pl.when(pl.program_id(0) % tiles_per_seq >= keep_from_tile)
    def _():
        bk_ref[0] = kb.T if seq_minor else kb
        bv_ref[0] = vb.T if seq_minor else vb


def _project(x2d, tab, weights, *, seq_len, n_keep, tm, seq_minor):
    n, d = x2d.shape
    g_attn, g_q, g_kv, w1, wq2, wkv2 = weights
    assert n % seq_len == 0 and seq_len % tm == 0 and n_keep % tm == 0
    tiles_per_seq = seq_len // tm
    keep_from_tile = (seq_len - n_keep) // tm
    n_seq = n // seq_len
    keep_tiles = n_keep // tm

    row = lambda width: pl.BlockSpec((tm, width), lambda i: (i, 0))
    keep_tile = lambda i: jnp.maximum(i % tiles_per_seq - keep_from_tile, 0)
    if seq_minor:
        kr_spec = pl.BlockSpec((1, ROPE_DIM, tm), lambda i: (i // tiles_per_seq, 0, i % tiles_per_seq))
        kr_shape = jax.ShapeDtypeStruct((n_seq, ROPE_DIM, seq_len), F32)
        keep_spec = pl.BlockSpec((1, MIX_B, tm), lambda i: (i // tiles_per_seq, 0, keep_tile(i)))
        keep_shape = jax.ShapeDtypeStruct((n_seq, MIX_B, keep_tiles * tm), F32)
    else:
        kr_spec = row(ROPE_DIM)
        kr_shape = jax.ShapeDtypeStruct((n, ROPE_DIM), F32)
        keep_spec = pl.BlockSpec((1, tm, MIX_B), lambda i: (i // tiles_per_seq, keep_tile(i), 0))
        keep_shape = jax.ShapeDtypeStruct((n_seq, keep_tiles * tm, MIX_B), F32)
    bf = lambda width: jax.ShapeDtypeStruct((n, width), BF16)
    out_shape = (
        bf(H_A * NOPE_DIM), bf(H_A * ROPE_DIM), bf(H_A * NOPE_DIM), bf(LANES), bf(MIX_A),
        bf(MIX_B), bf(MIX_B), bf(MIX_B),
        jax.ShapeDtypeStruct((n, KV_RANK), F32), kr_shape, keep_shape, keep_shape,
    )
    out_specs = (
        row(H_A * NOPE_DIM), row(H_A * ROPE_DIM), row(H_A * NOPE_DIM), row(LANES), row(MIX_A),
        row(MIX_B), row(MIX_B), row(MIX_B), row(KV_RANK), kr_spec, keep_spec, keep_spec,
    )
    in_specs = [
        row(d),
        pl.BlockSpec((tm, 2 * LANES), lambda i: (i % tiles_per_seq, 0)),
        _const_spec(g_attn.shape), _const_spec(g_q.shape), _const_spec(g_kv.shape),
        _const_spec(w1.shape), _const_spec(wq2.shape), _const_spec(wkv2.shape),
    ]
    return pl.pallas_call(
        functools.partial(_proj_kernel, tiles_per_seq=tiles_per_seq, keep_from_tile=keep_from_tile,
                          seq_minor=seq_minor),
        grid=(n // tm,),
        in_specs=in_specs,
        out_specs=out_specs,
        out_shape=out_shape,
        compiler_params=_compiler_params(("arbitrary",)),
        name="projection",
    )(x2d, tab, g_attn, g_q, g_kv, w1, wq2, wkv2)


def _pipelined_softmax(tiles, scores, attend, lookahead):
    ahead = [scores(tile) for tile in tiles[:lookahead]]
    pending = None
    for i, tile in enumerate(tiles):
        s = ahead.pop(0)
        if i + lookahead < len(tiles):
            ahead.append(scores(tiles[i + lookahead]))
        p = jnp.exp2(s - jnp.max(s, axis=0, keepdims=True))
        sums = jnp.sum(p, axis=0, keepdims=True)
        if pending is not None:
            attend(*pending)
        pending = (tile, p.astype(BF16), sums)
    attend(*pending)


def _pair_output(acc, sums, t):
    half = acc.shape[0] // 2
    o_t = jnp.concatenate([acc[:half, :t] / sums[:, :t], acc[half:, t:] / sums[:, t:]], axis=0)
    return o_t.T


def _mla_kernel(qn_ref, qr_ref, kn_ref, krt_ref, v_ref, o_ref, ve_ref, vo_ref, *, n_tiles):
    t = ATT_TILE
    n_pairs = ve_ref.shape[0]
    lane = _lane_iota()
    low = lane < V_DIM
    own_even = lax.broadcasted_iota(jnp.int32, (LANES, 1), 0) < V_DIM
    for pr in range(n_pairs):
        vt = v_ref[:, pr * LANES:(pr + 1) * LANES].astype(F32).T
        ve_ref[pr] = jnp.where(own_even, vt, 1.0).astype(BF16)
        vo_ref[pr] = jnp.where(own_even, 1.0, vt).astype(BF16)
    key_chunk = lax.broadcasted_iota(jnp.int32, (t, 2 * t), 0) // CHUNK
    query_chunk = (lax.broadcasted_iota(jnp.int32, (t, 2 * t), 1) % t) // CHUNK
    causal = key_chunk <= query_chunk

    def scores(unit):
        pr, qi = unit
        cols = slice(pr * LANES, (pr + 1) * LANES)
        r0, n_keys = qi * t, (qi + 1) * t
        qn = qn_ref[r0:r0 + t, cols]
        qr = qr_ref[r0:r0 + t, :]
        k2 = jnp.concatenate([kn_ref[0:n_keys, cols], krt_ref[0:n_keys, :]], axis=1)
        q_heads = []
        for hh in range(2):
            nope_mask = low if hh == 0 else jnp.logical_not(low)
            lo = (2 * pr + hh) * ROPE_DIM
            rope_mask = jnp.logical_and(lane >= lo, lane < lo + ROPE_DIM)
            q_heads.append(jnp.concatenate(
                [jnp.where(nope_mask, qn, jnp.zeros_like(qn)),
                 jnp.where(rope_mask, qr, jnp.zeros_like(qr))], axis=1))
        s = _dot_nt(k2, jnp.concatenate(q_heads, axis=0))
        s_diag = jnp.where(causal, s[r0:, :], NEG_INF)
        return s_diag if qi == 0 else jnp.concatenate([s[:r0, :], s_diag], axis=0)

    def attend(unit, p, _):
        pr, qi = unit
        r0, n_keys = qi * t, (qi + 1) * t
        acc_even = _dot(ve_ref[pr, :, 0:n_keys], p[:, :t])
        acc_odd = _dot(vo_ref[pr, :, 0:n_keys], p[:, t:])
        num = jnp.where(own_even, acc_even, acc_odd)
        sums = jnp.where(own_even, acc_odd, acc_even)
        den = jnp.concatenate([sums[V_DIM:], sums[:V_DIM]], axis=0)
        o_ref[r0:r0 + t, pr * LANES:(pr + 1) * LANES] = (num / den).T

    units = [(pr, qi) for pr in range(n_pairs) for qi in range(n_tiles)]
    _pipelined_softmax(units, scores, attend, lookahead=1)


def _mla_prompt(qn, qr, kn, krt, v, *, seq_len):
    n = qn.shape[0]
    assert seq_len % ATT_TILE == 0
    n_pairs = ROPE_PER_BLOCK // 2
    width = n_pairs * LANES
    blk = lambda w, f: pl.BlockSpec((seq_len, w), f)
    return pl.pallas_call(
        functools.partial(_mla_kernel, n_tiles=seq_len // ATT_TILE),
        grid=(n // seq_len, H_A // (2 * n_pairs)),
        in_specs=[
            blk(width, lambda b, j: (b, j)),
            blk(LANES, lambda b, j: (b, j)),
            blk(width, lambda b, j: (b, j)),
            blk(LANES, lambda b, j: (b, 0)),
            blk(width, lambda b, j: (b, j)),
        ],
        out_specs=blk(width, lambda b, j: (b, j)),
        out_shape=jax.ShapeDtypeStruct((n, MIX_A), F32),
        scratch_shapes=[pltpu.VMEM((n_pairs, LANES, seq_len), BF16),
                        pltpu.VMEM((n_pairs, LANES, seq_len), BF16)],
        compiler_params=_compiler_params(("arbitrary", "arbitrary")),
        name="mla_prompt",
    )(qn, qr, kn, krt, v)


def _band_kernel(q_ref, k_ref, v_ref, t_ref, o_ref, vt_ref, *, n_tiles):
    t = BAND_Q
    n_pairs = vt_ref.shape[0]
    low = _lane_iota() < HD_B
    for pr in range(n_pairs):
        vt_ref[pr] = v_ref[:, pr * LANES:(pr + 1) * LANES].astype(F32).T.astype(BF16)

    def window(g):
        return max((g + 1) * t - BAND_KEYS, 0), (g + 1) * t

    def scores(unit):
        pr, g = unit
        cols = slice(pr * LANES, (pr + 1) * LANES)
        k_lo, k_hi = window(g)
        q = q_ref[g * t:(g + 1) * t, cols]
        zero_q = jnp.zeros_like(q)
        q2 = jnp.concatenate([jnp.where(low, q, zero_q), jnp.where(low, zero_q, q)], axis=0)
        first = BAND_KEYS - (k_hi - k_lo)
        bias = jnp.concatenate([t_ref[2 * pr, first:BAND_KEYS, 0:t],
                                t_ref[2 * pr + 1, first:BAND_KEYS, 0:t]], axis=1)
        return _dot_nt(k_ref[k_lo:k_hi, cols], q2) + bias

    def attend(unit, p, sums):
        pr, g = unit
        k_lo, k_hi = window(g)
        o_ref[g * t:(g + 1) * t, pr * LANES:(pr + 1) * LANES] = _pair_output(
            _dot(vt_ref[pr, :, k_lo:k_hi], p), sums, t)

    units = [(pr, g) for pr in range(n_pairs) for g in range(n_tiles)]
    _pipelined_softmax(units, scores, attend, lookahead=2)


def _band_prompt(qb, kb, vb, table, *, seq_len):
    n = qb.shape[0]
    assert seq_len % BAND_Q == 0
    n_pairs = 2
    blk = pl.BlockSpec((seq_len, n_pairs * LANES), lambda b, j: (b, j))
    return pl.pallas_call(
        functools.partial(_band_kernel, n_tiles=seq_len // BAND_Q),
        grid=(n // seq_len, H_B // (2 * n_pairs)),
        in_specs=[blk, blk, blk,
                  pl.BlockSpec((2 * n_pairs,) + table.shape[1:], lambda b, j: (j, 0, 0))],
        out_specs=blk,
        out_shape=jax.ShapeDtypeStruct((n, MIX_B), F32),
        scratch_shapes=[pltpu.VMEM((n_pairs, LANES, seq_len), BF16)],
        compiler_params=_compiler_params(("arbitrary", "arbitrary")),
        name="band_prompt",
    )(qb, kb, vb, table)


def _sample_kernel(qn_ref, qr_ref, ckvn_ref, krtn_ref, qb_ref, kbn_ref, vbn_ref,
                   cckv_ref, ckrt_ref, cbkt_ref, cbvt_ref, t_ref, wukt_ref, wuv_ref,
                   oa_ref, ob_ref, *, n_tok):
    lane = _lane_iota()
    low = lane < V_DIM

    qn = qn_ref[...]
    qr = qr_ref[...]
    q_lat, q_rope = [], []
    for h in range(H_A):
        pair, par = divmod(h, 2)
        blk = qn[:, pair * LANES:(pair + 1) * LANES]
        head_mask = low if par == 0 else jnp.logical_not(low)
        q_h = jnp.where(head_mask, blk, jnp.zeros_like(blk))
        q_lat.append(_dot(q_h, wukt_ref[pair * LANES:(pair + 1) * LANES, :]))
        rblk, rpos = divmod(h, ROPE_PER_BLOCK)
        r = qr[:, rblk * LANES:(rblk + 1) * LANES]
        own = jnp.logical_and(lane >= rpos * ROPE_DIM, lane < (rpos + 1) * ROPE_DIM)
        q_rope.append(jnp.where(own, r, jnp.zeros_like(r)))
    q_lat = jnp.concatenate(q_lat, axis=0).astype(BF16)
    q_rope = jnp.concatenate(q_rope, axis=0)

    ckv_c = cckv_ref[0].astype(BF16)
    krt_c = jnp.concatenate([ckrt_ref[0].astype(BF16)] * ROPE_PER_BLOCK, axis=0)
    ckv_n = ckvn_ref[...].astype(BF16)
    s_c = _dot_nt(q_lat, ckv_c) + _dot(q_rope, krt_c)
    s_n = _dot_nt(q_lat, ckv_n) + _dot_nt(q_rope, krtn_ref[...])
    m = jnp.maximum(jnp.max(s_c, axis=1, keepdims=True), jnp.max(s_n, axis=1, keepdims=True))
    p_c = jnp.exp2(s_c - m)
    p_n = jnp.exp2(s_n - m)
    l = jnp.sum(p_c, axis=1, keepdims=True) + jnp.sum(p_n, axis=1, keepdims=True)
    o_lat = ((_dot(p_c.astype(BF16), ckv_c) + _dot(p_n.astype(BF16), ckv_n)) / l).astype(BF16)
    wuv = wuv_ref[...]
    col_head = lax.broadcasted_iota(jnp.int32, (1, MIX_A), 1) // V_DIM
    out_a = None
    for h in range(H_A):
        w_h = jnp.where(col_head == h, wuv, jnp.zeros_like(wuv))
        o_h = _dot(o_lat[h * n_tok:(h + 1) * n_tok], w_h)
        out_a = o_h if out_a is None else out_a + o_h
    oa_ref[...] = out_a

    n_cache = cbkt_ref.shape[2]
    qb = qb_ref[...]
    kb_n = kbn_ref[...]
    vb_n = vbn_ref[...]
    band_off = BAND_PAST - n_cache
    out_pairs = []
    for pair in range(H_B // 2):
        cols = slice(pair * LANES, (pair + 1) * LANES)
        blk = qb[:, cols]
        q2 = jnp.concatenate([jnp.where(low, blk, jnp.zeros_like(blk)),
                              jnp.where(low, jnp.zeros_like(blk), blk)], axis=0)
        bias = jnp.concatenate([t_ref[2 * pair], t_ref[2 * pair + 1]], axis=0)
        kt_c = cbkt_ref[0, cols, :].astype(BF16)
        vt_c = cbvt_ref[0, cols, :].astype(BF16)
        s_c = _dot(q2, kt_c) + bias[:, band_off:BAND_PAST]
        s_n = _dot_nt(q2, kb_n[:, cols]) + bias[:, BAND_PAST:BAND_PAST + n_tok]
        m = jnp.maximum(jnp.max(s_c, axis=1, keepdims=True), jnp.max(s_n, axis=1, keepdims=True))
        p_c = jnp.exp2(s_c - m)
        p_n = jnp.exp2(s_n - m)
        l = jnp.sum(p_c, axis=1, keepdims=True) + jnp.sum(p_n, axis=1, keepdims=True)
        o2 = (_dot_nt(p_c.astype(BF16), vt_c) + _dot(p_n.astype(BF16), vb_n[:, cols])) / l
        out_pairs.append(jnp.where(low, o2[0:n_tok], o2[n_tok:2 * n_tok]))
    ob_ref[...] = jnp.concatenate(out_pairs, axis=1)


def _sample_attention(proj, caches, table, wukt, wuv, *, n_streams, n_tok):
    qn, qr, _, krt, _, qb, kb, vb, ckv, _, _, _ = proj
    cache_ckv, cache_kr, cache_bk, cache_bv = caches
    n_band = cache_bk.shape[1]
    assert n_band <= BAND_PAST and BAND_PAST + n_tok <= table.shape[2]
    ckrt = jnp.swapaxes(cache_kr, 1, 2)
    cbkt = jnp.transpose(cache_bk, (0, 2, 3, 1)).reshape(n_streams, MIX_B, n_band)
    cbvt = jnp.transpose(cache_bv, (0, 2, 3, 1)).reshape(n_streams, MIX_B, n_band)
    tok = lambda width: pl.BlockSpec((n_tok, width), lambda b: (b, 0))
    per_stream = lambda a: pl.BlockSpec((1,) + a.shape[1:], lambda b: (b, 0, 0))
    out = jax.ShapeDtypeStruct((n_streams * n_tok, MIX_A), F32)
    return pl.pallas_call(
        functools.partial(_sample_kernel, n_tok=n_tok),
        grid=(n_streams,),
        in_specs=[
            tok(qn.shape[1]), tok(qr.shape[1]), tok(ckv.shape[1]), tok(krt.shape[1]),
            tok(qb.shape[1]), tok(kb.shape[1]), tok(vb.shape[1]),
            per_stream(cache_ckv), per_stream(ckrt), per_stream(cbkt), per_stream(cbvt),
            pl.BlockSpec((H_B, n_tok, table.shape[2]), lambda b: (0, 0, 0)),
            _const_spec(wukt.shape), _const_spec(wuv.shape),
        ],
        out_specs=(tok(MIX_A), tok(MIX_B)),
        out_shape=(out, out),
        compiler_params=_compiler_params(("arbitrary",)),
        name="sample_attention",
    )(qn, qr, ckv, krt, qb, kb, vb, cache_ckv, ckrt, cbkt, cbvt, table, wukt, wuv)


def _ffn_chunks(d_ff):
    chunks, start = [], 0
    while start < d_ff:
        size = min(2 * MXU_DIM, d_ff - start)
        chunks.append((start, size))
        start += size
    return chunks


def _out_kernel(x_ref, oa_ref, ob_ref, goa_ref, gob_ref, gffn_ref, gfin_ref,
                wout_ref, wg_ref, wu_ref, wd_ref, y_ref, *, sub_rows):
    chunks = _ffn_chunks(wg_ref.shape[1])

    def mixed(rows):
        mix = jnp.concatenate([_rms(oa_ref[rows, :], goa_ref[...]), _rms(ob_ref[rows, :], gob_ref[...])],
                              axis=1).astype(BF16)
        x1 = x_ref[rows, :] + _dot(mix, wout_ref[...])
        return x1, _rms(x1, gffn_ref[...]).astype(BF16)

    def ffn_part(h, chunk):
        start, size = chunk
        gate = _dot(h, wg_ref[:, start:start + size])
        up = _dot(h, wu_ref[:, start:start + size])
        act = (jax.nn.silu(gate) * up).astype(BF16)
        return _dot(act, wd_ref[start:start + size, :])

    n_sub = x_ref.shape[0] // sub_rows
    tiles = [slice(r * sub_rows, (r + 1) * sub_rows) for r in range(n_sub)]
    state = mixed(tiles[0])
    for r, rows in enumerate(tiles):
        x1, h = state
        ffn = ffn_part(h, chunks[0])
        if r + 1 < n_sub:
            state = mixed(tiles[r + 1])
        for chunk in chunks[1:]:
            ffn = ffn + ffn_part(h, chunk)
        y_ref[rows, :] = _rms(x1 + ffn, gfin_ref[...])


def _output(x2d, oa, ob, weights, *, tm, sub_rows):
    n, d = x2d.shape
    assert n % tm == 0 and tm % sub_rows == 0
    row = lambda width: pl.BlockSpec((tm, width), lambda i: (i, 0))
    return pl.pallas_call(
        functools.partial(_out_kernel, sub_rows=sub_rows),
        grid=(n // tm,),
        in_specs=[row(d), row(MIX_A), row(MIX_B)] + [_const_spec(w.shape) for w in weights],
        out_specs=row(d),
        out_shape=jax.ShapeDtypeStruct((n, d), F32),
        compiler_params=_compiler_params(("arbitrary",)),
        name="output_ffn",
    )(x2d, oa, ob, *weights)


def _rotate_half_cols(w):
    half = ROPE_DIM // 2
    return jnp.concatenate([-w[..., half:], w[..., :half]], axis=-1)


def _layout_weights(w_in, w_uq, w_uk, w_uv):
    d = w_in.shape[0]
    o = 0
    w_cq = w_in[:, o:o + Q_RANK]; o += Q_RANK
    w_ckv = w_in[:, o:o + KV_RANK]; o += KV_RANK
    w_kr = w_in[:, o:o + ROPE_DIM]; o += ROPE_DIM
    w_qb = w_in[:, o:o + MIX_B]; o += MIX_B
    w_kvb = w_in[:, o:]
    w1 = jnp.concatenate(
        [w_cq, w_ckv, jnp.tile(w_kr, (1, ROPE_PER_BLOCK)),
         jnp.tile(_rotate_half_cols(w_kr), (1, ROPE_PER_BLOCK)),
         w_qb, w_kvb], axis=1).astype(BF16)
    uq = w_uq.reshape(Q_RANK, H_A, NOPE_DIM + ROPE_DIM)
    uq_rope = uq[:, :, NOPE_DIM:]
    wq2 = jnp.concatenate(
        [uq[:, :, :NOPE_DIM].reshape(Q_RANK, H_A * NOPE_DIM),
         uq_rope.reshape(Q_RANK, H_A * ROPE_DIM),
         _rotate_half_cols(uq_rope).reshape(Q_RANK, H_A * ROPE_DIM)], axis=1).astype(BF16)
    uk = w_uk.reshape(KV_RANK, H_A * NOPE_DIM)
    uv = w_uv.reshape(KV_RANK, H_A * V_DIM)
    wkv2 = jnp.concatenate([uk, uv], axis=1).astype(BF16)
    return w1, wq2, wkv2, uk.T.astype(BF16), uv.astype(BF16)


def _rope_table(pos):
    inv = ROPE_THETA ** (-jnp.arange(0, ROPE_DIM, 2, dtype=F32) / ROPE_DIM)
    ang = pos.astype(F32)[:, None] * inv[None, :]
    cos = jnp.tile(jnp.cos(ang), (1, 2 * ROPE_PER_BLOCK))
    sin = jnp.tile(jnp.sin(ang), (1, 2 * ROPE_PER_BLOCK))
    return jnp.concatenate([cos, sin], axis=1)


def _row_tile(n, cap):
    tm = min(n, cap)
    assert n % tm == 0
    return tm


def kernel(x_prompt, x_sample, cache_mla_ckv, cache_mla_krope, cache_band_k, cache_band_v,
           w_in, g_attn, g_q, w_uq, g_kv, w_uk, w_uv, rel_bias, g_out_a, g_out_b, w_out,
           g_ffn, w_gate, w_up, w_down, g_final):
    depth = w_in.shape[0]
    assert depth == 1, "single-layer trunk"
    batch, seq, d = x_prompt.shape
    n_streams, n_tok, _ = x_sample.shape
    past = cache_mla_ckv.shape[2]

    w1, wq2, wkv2, wukt, wuv = _layout_weights(w_in[0], w_uq[0], w_uk[0], w_uv[0])
    proj_w = (g_attn, g_q, g_kv, w1, wq2, wkv2)
    out_w = (g_out_a, g_out_b, g_ffn, g_final[None, :], w_out[0].astype(BF16),
             w_gate[0].astype(BF16), w_up[0].astype(BF16), w_down[0].astype(BF16))
    table, table_k = _bias_table(rel_bias[0])

    n_keep = min(BAND_PAST, seq)
    xp = x_prompt.reshape(batch * seq, d)
    tm = _row_tile(n_keep, 512)
    proj = _project(xp, _rope_table(jnp.arange(seq, dtype=jnp.int32)), proj_w,
                    seq_len=seq, n_keep=n_keep, tm=tm, seq_minor=True)
    qn, qr, kn, krt, v, qb, kb, vb, ckv_p, krt_p, bkt_p, bvt_p = proj
    kr_p = jnp.swapaxes(krt_p, 1, 2)
    band_state = lambda s: jnp.transpose(s.reshape(batch, H_B, HD_B, n_keep), (0, 3, 1, 2))
    bk_p, bv_p = band_state(bkt_p), band_state(bvt_p)
    oa = _mla_prompt(qn, qr, kn, krt, v, seq_len=seq)
    ob = _band_prompt(qb, kb, vb, table_k, seq_len=seq)
    out_tm = 2 * tm if (batch * seq) % (2 * tm) == 0 else tm
    y_prompt = _output(xp, oa, ob, out_w, tm=out_tm, sub_rows=tm).reshape(batch, seq, d)

    n_s = n_streams * n_tok
    xs = x_sample.reshape(n_s, d)
    pos_s = past + jnp.tile(jnp.arange(n_tok, dtype=jnp.int32), n_streams)
    proj_s = _project(xs, _rope_table(pos_s), proj_w, seq_len=n_s, n_keep=n_s, tm=n_s,
                      seq_minor=False)
    oa_s, ob_s = _sample_attention(
        proj_s, (cache_mla_ckv[0], cache_mla_krope[0], cache_band_k[0], cache_band_v[0]),
        table, wukt, wuv, n_streams=n_streams, n_tok=n_tok)
    y_sample = _output(xs, oa_s, ob_s, out_w, tm=n_s, sub_rows=n_s).reshape(n_streams, n_tok, d)
    ckv_s, kr_s, bk_s, bv_s = proj_s[8:12]

    return (
        y_prompt, y_sample,
        ckv_p.reshape(1, batch, seq, KV_RANK), kr_p.reshape(1, batch, seq, ROPE_DIM),
        bk_p.reshape(1, batch, n_keep, H_B, HD_B), bv_p.reshape(1, batch, n_keep, H_B, HD_B),
        ckv_s.reshape(1, n_streams, n_tok, KV_RANK), kr_s.reshape(1, n_streams, n_tok, ROPE_DIM),
        bk_s.reshape(1, n_streams, n_tok, H_B, HD_B), bv_s.reshape(1, n_streams, n_tok, H_B, HD_B),
    )
```

```python
import functools

import jax
import jax.numpy as jnp
from jax import lax
from jax.experimental import pallas as pl
from jax.experimental.pallas import tpu as pltpu

CHUNK = 64
EPS = 1e-6
NEG_INF = -1e30
H_A = 8
NOPE_DIM = 64
ROPE_DIM = 32
V_DIM = 64
Q_RANK = 256
KV_RANK = 256
ROPE_THETA = 10000.0
MLA_SCALE = (NOPE_DIM + ROPE_DIM) ** -0.5
H_B = 8
HD_B = 64
N_PREV_CHUNKS = 8
BAND_PAST = N_PREV_CHUNKS * CHUNK
REL_CLIP = 256
BAND_SCALE = HD_B ** -0.5
LOG2E = 1.4426950408889634
MIX_A = H_A * V_DIM
MIX_B = H_B * HD_B

LANES = 128
MXU_DIM = 256
VMEM_LIMIT_BYTES = 56 * 1024 * 1024

ATT_TILE = 4 * CHUNK
BAND_TILES = N_PREV_CHUNKS * CHUNK // ATT_TILE + 1
BAND_Q = 2 * CHUNK
BAND_KEYS = N_PREV_CHUNKS * CHUNK + BAND_Q
PAIR = 2 * V_DIM
ROPE_PER_BLOCK = LANES // ROPE_DIM

F32 = jnp.float32
BF16 = jnp.bfloat16


def _dot(a, b):
    return jnp.dot(a, b, preferred_element_type=F32)


def _dot_nt(a, b):
    return lax.dot_general(a, b, (((1,), (1,)), ((), ())), preferred_element_type=F32)


def _rms(x, g):
    return x * lax.rsqrt(jnp.mean(x * x, axis=-1, keepdims=True) + EPS) * g


def _lane_iota(width=LANES):
    return lax.broadcasted_iota(jnp.int32, (1, width), 1)


def _compiler_params(semantics):
    return pltpu.CompilerParams(dimension_semantics=semantics, vmem_limit_bytes=VMEM_LIMIT_BYTES)


def _const_spec(shape):
    nd = len(shape)
    return pl.BlockSpec(shape, lambda *_: (0,) * nd, pipeline_mode=pl.Buffered(1))


def _bias_kernel(g_ref, t_ref, tk_ref):
    rows, width = t_ref.shape[1], g_ref.shape[2]
    x = jnp.broadcast_to(g_ref[0], (rows, width))
    x = pltpu.roll(x, 0, 1, stride=1, stride_axis=0)
    x = x[:, width - t_ref.shape[2]:]
    qc = lax.broadcasted_iota(jnp.int32, x.shape, 0) // CHUNK
    kc = lax.broadcasted_iota(jnp.int32, x.shape, 1) // CHUNK
    visible = jnp.logical_and(kc >= qc, kc <= qc + N_PREV_CHUNKS)
    table = jnp.where(visible, x * LOG2E, NEG_INF)
    t_ref[0] = table
    tk_ref[0] = table.T


def _bias_table(rel_bias):
    n_keys = BAND_TILES * ATT_TILE
    width = n_keys + ATT_TILE
    n_const = width - 2 * REL_CLIP + 1
    g = jnp.concatenate(
        [jnp.broadcast_to(rel_bias[:, 2 * REL_CLIP:], (H_B, n_const)),
         rel_bias[:, 2 * REL_CLIP - 1:0:-1]], axis=1)
    g = g.reshape(H_B, 1, width)
    return pl.pallas_call(
        _bias_kernel,
        grid=(H_B,),
        in_specs=[pl.BlockSpec((1, 1, width), lambda h: (h, 0, 0))],
        out_specs=(pl.BlockSpec((1, ATT_TILE, n_keys), lambda h: (h, 0, 0)),
                   pl.BlockSpec((1, n_keys, ATT_TILE), lambda h: (h, 0, 0))),
        out_shape=(jax.ShapeDtypeStruct((H_B, ATT_TILE, n_keys), F32),
                   jax.ShapeDtypeStruct((H_B, n_keys, ATT_TILE), F32)),
        compiler_params=_compiler_params(("arbitrary",)),
        name="bias_table",
    )(g)


def _proj_kernel(x_ref, tab_ref, ga_ref, gq_ref, gkv_ref, w1_ref, wq2_ref, wkv2_ref,
                 qn_ref, qr_ref, kn_ref, krt_ref, v_ref, qb_ref, kb_ref, vb_ref,
                 ckv_ref, kr_ref, bk_ref, bv_ref, *, sub_rows, steps_per_seq, keep_from_sub,
                 seq_minor):
    n_nope = H_A * NOPE_DIM
    n_rope = H_A * ROPE_DIM
    o_kv = Q_RANK
    o_kr = o_kv + KV_RANK
    o_b = o_kr + 2 * LANES
    n_sub = x_ref.shape[0] // sub_rows
    tiles = [slice(r * sub_rows, (r + 1) * sub_rows) for r in range(n_sub)]
    step_in_seq = pl.program_id(0) % steps_per_seq

    def normed(rows):
        return _rms(x_ref[rows, :], ga_ref[...]).astype(BF16)

    def first_dots(xn):
        return (_dot(xn, w1_ref[:, 0:Q_RANK]),
                _dot(xn, w1_ref[:, o_kv:o_kv + KV_RANK]),
                _dot(xn, w1_ref[:, o_kr:o_kr + 2 * LANES]),
                _dot(xn, w1_ref[:, o_b:o_b + 3 * MIX_B]))

    def finish(r, rows, c_q, c_kv, krr, qkv):
        cos = tab_ref[rows, 0:LANES]
        sin = tab_ref[rows, LANES:2 * LANES]

        qb_ref[rows, :] = (qkv[:, 0:MIX_B] * (BAND_SCALE * LOG2E)).astype(BF16)
        kb = qkv[:, MIX_B:2 * MIX_B]
        vb = qkv[:, 2 * MIX_B:]
        kb_ref[rows, :] = kb.astype(BF16)
        vb_ref[rows, :] = vb.astype(BF16)

        cqn = _rms(c_q, gq_ref[...]).astype(BF16)
        q2 = _dot(cqn, wq2_ref[...])
        qn_ref[rows, :] = (q2[:, 0:n_nope] * (MLA_SCALE * LOG2E)).astype(BF16)
        cos2 = jnp.concatenate([cos] * (n_rope // LANES), axis=1)
        sin2 = jnp.concatenate([sin] * (n_rope // LANES), axis=1)
        q_rope = q2[:, n_nope:n_nope + n_rope] * cos2 + q2[:, n_nope + n_rope:] * sin2
        qr_ref[rows, :] = (q_rope * (MLA_SCALE * LOG2E)).astype(BF16)

        ckvn = _rms(c_kv, gkv_ref[...])
        ckv_ref[rows, :] = ckvn
        kv2 = _dot(ckvn.astype(BF16), wkv2_ref[...])
        kn_ref[rows, :] = kv2[:, 0:n_nope].astype(BF16)
        v_ref[rows, :] = kv2[:, n_nope:].astype(BF16)

        k_rope = krr[:, 0:LANES] * cos + krr[:, LANES:] * sin
        krt_ref[rows, :] = k_rope.astype(BF16)
        if seq_minor:
            kr_ref[0, :, rows] = k_rope.T[0:ROPE_DIM, :]
        else:
            kr_ref[rows, :] = k_rope[:, 0:ROPE_DIM]

        if n_sub == 1 or r == n_sub - 1:
            @pl.when(step_in_seq * n_sub + r >= keep_from_sub)
            def _():
                bk_ref[0] = kb.T if seq_minor else kb
                bv_ref[0] = vb.T if seq_minor else vb

    xn = normed(tiles[0])
    for r, rows in enumerate(tiles):
        heads = first_dots(xn)
        if r + 1 < n_sub:
            xn = normed(tiles[r + 1])
        finish(r, rows, *heads)


def _project(x2d, tab, weights, *, seq_len, n_keep, tm, sub_rows, seq_minor):
    n, d = x2d.shape
    g_attn, g_q, g_kv, w1, wq2, wkv2 = weights
    assert n % seq_len == 0 and seq_len % tm == 0 and tm % sub_rows == 0 and n_keep % sub_rows == 0
    n_sub = tm // sub_rows
    assert n_sub == 1 or n_keep == sub_rows, "several row tiles per step: one kept tile per sequence"
    steps_per_seq = seq_len // tm
    keep_from_sub = (seq_len - n_keep) // sub_rows
    n_seq = n // seq_len
    keep_rows = n_keep

    row = lambda width: pl.BlockSpec((tm, width), lambda i: (i, 0))
    keep_tile = lambda i: jnp.maximum((i % steps_per_seq) * n_sub + n_sub - 1 - keep_from_sub, 0)
    if seq_minor:
        kr_spec = pl.BlockSpec((1, ROPE_DIM, tm), lambda i: (i // steps_per_seq, 0, i % steps_per_seq))
        kr_shape = jax.ShapeDtypeStruct((n_seq, ROPE_DIM, seq_len), F32)
        keep_spec = pl.BlockSpec((1, MIX_B, sub_rows), lambda i: (i // steps_per_seq, 0, keep_tile(i)))
        keep_shape = jax.ShapeDtypeStruct((n_seq, MIX_B, keep_rows), F32)
    else:
        kr_spec = row(ROPE_DIM)
        kr_shape = jax.ShapeDtypeStruct((n, ROPE_DIM), F32)
        keep_spec = pl.BlockSpec((1, sub_rows, MIX_B), lambda i: (i // steps_per_seq, keep_tile(i), 0))
        keep_shape = jax.ShapeDtypeStruct((n_seq, keep_rows, MIX_B), F32)
    bf = lambda width: jax.ShapeDtypeStruct((n, width), BF16)
    out_shape = (
        bf(H_A * NOPE_DIM), bf(H_A * ROPE_DIM), bf(H_A * NOPE_DIM), bf(LANES), bf(MIX_A),
        bf(MIX_B), bf(MIX_B), bf(MIX_B),
        jax.ShapeDtypeStruct((n, KV_RANK), F32), kr_shape, keep_shape, keep_shape,
    )
    out_specs = (
        row(H_A * NOPE_DIM), row(H_A * ROPE_DIM), row(H_A * NOPE_DIM), row(LANES), row(MIX_A),
        row(MIX_B), row(MIX_B), row(MIX_B), row(KV_RANK), kr_spec, keep_spec, keep_spec,
    )
    in_specs = [
        row(d),
        pl.BlockSpec((tm, 2 * LANES), lambda i: (i % steps_per_seq, 0)),
        _const_spec(g_attn.shape), _const_spec(g_q.shape), _const_spec(g_kv.shape),
        _const_spec(w1.shape), _const_spec(wq2.shape), _const_spec(wkv2.shape),
    ]
    return pl.pallas_call(
        functools.partial(_proj_kernel, sub_rows=sub_rows, steps_per_seq=steps_per_seq,
                          keep_from_sub=keep_from_sub, seq_minor=seq_minor),
        grid=(n // tm,),
        in_specs=in_specs,
        out_specs=out_specs,
        out_shape=out_shape,
        compiler_params=_compiler_params(("arbitrary",)),
        name="projection",
    )(x2d, tab, g_attn, g_q, g_kv, w1, wq2, wkv2)


def _pipelined_softmax(tiles, scores, attend, lookahead):
    ahead = [scores(tile) for tile in tiles[:lookahead]]
    pending = None
    for i, tile in enumerate(tiles):
        s = ahead.pop(0)
        if i + lookahead < len(tiles):
            ahead.append(scores(tiles[i + lookahead]))
        p = jnp.exp2(s - jnp.max(s, axis=0, keepdims=True))
        sums = jnp.sum(p, axis=0, keepdims=True)
        if pending is not None:
            attend(*pending)
        pending = (tile, p.astype(BF16), sums)
    attend(*pending)


def _pair_output(acc, sums, t):
    half = acc.shape[0] // 2
    o_t = jnp.concatenate([acc[:half, :t] / sums[:, :t], acc[half:, t:] / sums[:, t:]], axis=0)
    return o_t.T


def _mla_kernel(qn_ref, qr_ref, kn_ref, krt_ref, v_ref, o_ref, ve_ref, vo_ref, *, n_tiles):
    t = ATT_TILE
    n_pairs = ve_ref.shape[0]
    lane = _lane_iota()
    low = lane < V_DIM
    own_even = lax.broadcasted_iota(jnp.int32, (LANES, 1), 0) < V_DIM
    for pr in range(n_pairs):
        vt = v_ref[:, pr * LANES:(pr + 1) * LANES].astype(F32).T
        ve_ref[pr] = jnp.where(own_even, vt, 1.0).astype(BF16)
        vo_ref[pr] = jnp.where(own_even, 1.0, vt).astype(BF16)
    key_chunk = lax.broadcasted_iota(jnp.int32, (t, 2 * t), 0) // CHUNK
    query_chunk = (lax.broadcasted_iota(jnp.int32, (t, 2 * t), 1) % t) // CHUNK
    causal = key_chunk <= query_chunk

    def scores(unit):
        pr, qi = unit
        cols = slice(pr * LANES, (pr + 1) * LANES)
        r0, n_keys = qi * t, (qi + 1) * t
        qn = qn_ref[r0:r0 + t, cols]
        qr = qr_ref[r0:r0 + t, :]
        k2 = jnp.concatenate([kn_ref[0:n_keys, cols], krt_ref[0:n_keys, :]], axis=1)
        q_heads = []
        for hh in range(2):
            nope_mask = low if hh == 0 else jnp.logical_not(low)
            lo = (2 * pr + hh) * ROPE_DIM
            rope_mask = jnp.logical_and(lane >= lo, lane < lo + ROPE_DIM)
            q_heads.append(jnp.concatenate(
                [jnp.where(nope_mask, qn, jnp.zeros_like(qn)),
                 jnp.where(rope_mask, qr, jnp.zeros_like(qr))], axis=1))
        s = _dot_nt(k2, jnp.concatenate(q_heads, axis=0))
        s_diag = jnp.where(causal, s[r0:, :], NEG_INF)
        return s_diag if qi == 0 else jnp.concatenate([s[:r0, :], s_diag], axis=0)

    def attend(unit, p, _):
        pr, qi = unit
        r0, n_keys = qi * t, (qi + 1) * t
        acc_even = _dot(ve_ref[pr, :, 0:n_keys], p[:, :t])
        acc_odd = _dot(vo_ref[pr, :, 0:n_keys], p[:, t:])
        num = jnp.where(own_even, acc_even, acc_odd)
        sums = jnp.where(own_even, acc_odd, acc_even)
        den = jnp.concatenate([sums[V_DIM:], sums[:V_DIM]], axis=0)
        o_ref[r0:r0 + t, pr * LANES:(pr + 1) * LANES] = (num / den).T

    units = [(pr, qi) for pr in range(n_pairs) for qi in range(n_tiles)]
    _pipelined_softmax(units, scores, attend, lookahead=1)


def _mla_prompt(qn, qr, kn, krt, v, *, seq_len):
    n = qn.shape[0]
    assert seq_len % ATT_TILE == 0
    n_pairs = ROPE_PER_BLOCK // 2
    width = n_pairs * LANES
    blk = lambda w, f: pl.BlockSpec((seq_len, w), f)
    return pl.pallas_call(
        functools.partial(_mla_kernel, n_tiles=seq_len // ATT_TILE),
        grid=(n // seq_len, H_A // (2 * n_pairs)),
        in_specs=[
            blk(width, lambda b, j: (b, j)),
            blk(LANES, lambda b, j: (b, j)),
            blk(width, lambda b, j: (b, j)),
            blk(LANES, lambda b, j: (b, 0)),
            blk(width, lambda b, j: (b, j)),
        ],
        out_specs=blk(width, lambda b, j: (b, j)),
        out_shape=jax.ShapeDtypeStruct((n, MIX_A), F32),
        scratch_shapes=[pltpu.VMEM((n_pairs, LANES, seq_len), BF16),
                        pltpu.VMEM((n_pairs, LANES, seq_len), BF16)],
        compiler_params=_compiler_params(("arbitrary", "arbitrary")),
        name="mla_prompt",
    )(qn, qr, kn, krt, v)


def _band_kernel(q_ref, k_ref, v_ref, t_ref, o_ref, vt_ref, *, n_tiles):
    t = BAND_Q
    n_pairs = vt_ref.shape[0]
    low = _lane_iota() < HD_B
    for pr in range(n_pairs):
        vt_ref[pr] = v_ref[:, pr * LANES:(pr + 1) * LANES].astype(F32).T.astype(BF16)

    def window(g):
        return max((g + 1) * t - BAND_KEYS, 0), (g + 1) * t

    def scores(unit):
        pr, g = unit
        cols = slice(pr * LANES, (pr + 1) * LANES)
        k_lo, k_hi = window(g)
        q = q_ref[g * t:(g + 1) * t, cols]
        zero_q = jnp.zeros_like(q)
        q2 = jnp.concatenate([jnp.where(low, q, zero_q), jnp.where(low, zero_q, q)], axis=0)
        first = BAND_KEYS - (k_hi - k_lo)
        bias = jnp.concatenate([t_ref[2 * pr, first:BAND_KEYS, 0:t],
                                t_ref[2 * pr + 1, first:BAND_KEYS, 0:t]], axis=1)
        return _dot_nt(k_ref[k_lo:k_hi, cols], q2) + bias

    def attend(unit, p, sums):
        pr, g = unit
        k_lo, k_hi = window(g)
        o_ref[g * t:(g + 1) * t, pr * LANES:(pr + 1) * LANES] = _pair_output(
            _dot(vt_ref[pr, :, k_lo:k_hi], p), sums, t)

    units = [(pr, g) for pr in range(n_pairs) for g in range(n_tiles)]
    _pipelined_softmax(units, scores, attend, lookahead=2)


def _band_prompt(qb, kb, vb, table, *, seq_len):
    n = qb.shape[0]
    assert seq_len % BAND_Q == 0
    n_pairs = 2
    blk = pl.BlockSpec((seq_len, n_pairs * LANES), lambda b, j: (b, j))
    return pl.pallas_call(
        functools.partial(_band_kernel, n_tiles=seq_len // BAND_Q),
        grid=(n // seq_len, H_B // (2 * n_pairs)),
        in_specs=[blk, blk, blk,
                  pl.BlockSpec((2 * n_pairs,) + table.shape[1:], lambda b, j: (j, 0, 0))],
        out_specs=blk,
        out_shape=jax.ShapeDtypeStruct((n, MIX_B), F32),
        scratch_shapes=[pltpu.VMEM((n_pairs, LANES, seq_len), BF16)],
        compiler_params=_compiler_params(("arbitrary", "arbitrary")),
        name="band_prompt",
    )(qb, kb, vb, table)


def _sample_kernel(qn_ref, qr_ref, ckvn_ref, krtn_ref, qb_ref, kbn_ref, vbn_ref,
                   cckv_ref, ckrt_ref, cbkt_ref, cbvt_ref, t_ref, wukt_ref, wuv_ref,
                   oa_ref, ob_ref, *, n_tok):
    lane = _lane_iota()
    low = lane < V_DIM

    qn = qn_ref[...]
    qr = qr_ref[...]
    q_lat, q_rope = [], []
    for h in range(H_A):
        pair, par = divmod(h, 2)
        blk = qn[:, pair * LANES:(pair + 1) * LANES]
        head_mask = low if par == 0 else jnp.logical_not(low)
        q_h = jnp.where(head_mask, blk, jnp.zeros_like(blk))
        q_lat.append(_dot(q_h, wukt_ref[pair * LANES:(pair + 1) * LANES, :]))
        rblk, rpos = divmod(h, ROPE_PER_BLOCK)
        r = qr[:, rblk * LANES:(rblk + 1) * LANES]
        own = jnp.logical_and(lane >= rpos * ROPE_DIM, lane < (rpos + 1) * ROPE_DIM)
        q_rope.append(jnp.where(own, r, jnp.zeros_like(r)))
    q_lat = jnp.concatenate(q_lat, axis=0).astype(BF16)
    q_rope = jnp.concatenate(q_rope, axis=0)

    ckv_c = cckv_ref[0].astype(BF16)
    krt_c = jnp.concatenate([ckrt_ref[0].astype(BF16)] * ROPE_PER_BLOCK, axis=0)
    ckv_n = ckvn_ref[...].astype(BF16)
    s_c = _dot_nt(q_lat, ckv_c) + _dot(q_rope, krt_c)
    s_n = _dot_nt(q_lat, ckv_n) + _dot_nt(q_rope, krtn_ref[...])
    m = jnp.maximum(jnp.max(s_c, axis=1, keepdims=True), jnp.max(s_n, axis=1, keepdims=True))
    p_c = jnp.exp2(s_c - m)
    p_n = jnp.exp2(s_n - m)
    l = jnp.sum(p_c, axis=1, keepdims=True) + jnp.sum(p_n, axis=1, keepdims=True)
    o_lat = ((_dot(p_c.astype(BF16), ckv_c) + _dot(p_n.astype(BF16), ckv_n)) / l).astype(BF16)
    wuv = wuv_ref[...]
    col_head = lax.broadcasted_iota(jnp.int32, (1, MIX_A), 1) // V_DIM
    out_a = None
    for h in range(H_A):
        w_h = jnp.where(col_head == h, wuv, jnp.zeros_like(wuv))
        o_h = _dot(o_lat[h * n_tok:(h + 1) * n_tok], w_h)
        out_a = o_h if out_a is None else out_a + o_h
    oa_ref[...] = out_a

    n_cache = cbkt_ref.shape[2]
    qb = qb_ref[...]
    kb_n = kbn_ref[...]
    vb_n = vbn_ref[...]
    band_off = BAND_PAST - n_cache
    out_pairs = []
    for pair in range(H_B // 2):
        cols = slice(pair * LANES, (pair + 1) * LANES)
        blk = qb[:, cols]
        q2 = jnp.concatenate([jnp.where(low, blk, jnp.zeros_like(blk)),
                              jnp.where(low, jnp.zeros_like(blk), blk)], axis=0)
        bias = jnp.concatenate([t_ref[2 * pair], t_ref[2 * pair + 1]], axis=0)
        kt_c = cbkt_ref[0, cols, :].astype(BF16)
        vt_c = cbvt_ref[0, cols, :].astype(BF16)
        s_c = _dot(q2, kt_c) + bias[:, band_off:BAND_PAST]
        s_n = _dot_nt(q2, kb_n[:, cols]) + bias[:, BAND_PAST:BAND_PAST + n_tok]
        m = jnp.maximum(jnp.max(s_c, axis=1, keepdims=True), jnp.max(s_n, axis=1, keepdims=True))
        p_c = jnp.exp2(s_c - m)
        p_n = jnp.exp2(s_n - m)
        l = jnp.sum(p_c, axis=1, keepdims=True) + jnp.sum(p_n, axis=1, keepdims=True)
        o2 = (_dot_nt(p_c.astype(BF16), vt_c) + _dot(p_n.astype(BF16), vb_n[:, cols])) / l
        out_pairs.append(jnp.where(low, o2[0:n_tok], o2[n_tok:2 * n_tok]))
    ob_ref[...] = jnp.concatenate(out_pairs, axis=1)


def _sample_attention(proj, caches, table, wukt, wuv, *, n_streams, n_tok):
    qn, qr, _, krt, _, qb, kb, vb, ckv, _, _, _ = proj
    cache_ckv, cache_kr, cache_bk, cache_bv = caches
    n_band = cache_bk.shape[1]
    assert n_band <= BAND_PAST and BAND_PAST + n_tok <= table.shape[2]
    ckrt = jnp.swapaxes(cache_kr, 1, 2)
    cbkt = jnp.transpose(cache_bk, (0, 2, 3, 1)).reshape(n_streams, MIX_B, n_band)
    cbvt = jnp.transpose(cache_bv, (0, 2, 3, 1)).reshape(n_streams, MIX_B, n_band)
    tok = lambda width: pl.BlockSpec((n_tok, width), lambda b: (b, 0))
    per_stream = lambda a: pl.BlockSpec((1,) + a.shape[1:], lambda b: (b, 0, 0))
    out = jax.ShapeDtypeStruct((n_streams * n_tok, MIX_A), F32)
    return pl.pallas_call(
        functools.partial(_sample_kernel, n_tok=n_tok),
        grid=(n_streams,),
        in_specs=[
            tok(qn.shape[1]), tok(qr.shape[1]), tok(ckv.shape[1]), tok(krt.shape[1]),
            tok(qb.shape[1]), tok(kb.shape[1]), tok(vb.shape[1]),
            per_stream(cache_ckv), per_stream(ckrt), per_stream(cbkt), per_stream(cbvt),
            pl.BlockSpec((H_B, n_tok, table.shape[2]), lambda b: (0, 0, 0)),
            _const_spec(wukt.shape), _const_spec(wuv.shape),
        ],
        out_specs=(tok(MIX_A), tok(MIX_B)),
        out_shape=(out, out),
        compiler_params=_compiler_params(("arbitrary",)),
        name="sample_attention",
    )(qn, qr, ckv, krt, qb, kb, vb, cache_ckv, ckrt, cbkt, cbvt, table, wukt, wuv)


def _ffn_chunks(d_ff):
    chunks, start = [], 0
    while start < d_ff:
        size = min(2 * MXU_DIM, d_ff - start)
        chunks.append((start, size))
        start += size
    return chunks


def _out_kernel(x_ref, oa_ref, ob_ref, goa_ref, gob_ref, gffn_ref, gfin_ref,
                wout_ref, wg_ref, wu_ref, wd_ref, y_ref, *, sub_rows):
    chunks = _ffn_chunks(wg_ref.shape[1])

    def mixed(rows):
        mix = jnp.concatenate([_rms(oa_ref[rows, :], goa_ref[...]), _rms(ob_ref[rows, :], gob_ref[...])],
                              axis=1).astype(BF16)
        x1 = x_ref[rows, :] + _dot(mix, wout_ref[...])
        return x1, _rms(x1, gffn_ref[...]).astype(BF16)

    def ffn_part(h, chunk):
        start, size = chunk
        gate = _dot(h, wg_ref[:, start:start + size])
        up = _dot(h, wu_ref[:, start:start + size])
        act = (jax.nn.silu(gate) * up).astype(BF16)
        return _dot(act, wd_ref[start:start + size, :])

    n_sub = x_ref.shape[0] // sub_rows
    tiles = [slice(r * sub_rows, (r + 1) * sub_rows) for r in range(n_sub)]
    state = mixed(tiles[0])
    for r, rows in enumerate(tiles):
        x1, h = state
        ffn = ffn_part(h, chunks[0])
        if r + 1 < n_sub:
            state = mixed(tiles[r + 1])
        for chunk in chunks[1:]:
            ffn = ffn + ffn_part(h, chunk)
        y_ref[rows, :] = _rms(x1 + ffn, gfin_ref[...])


def _output(x2d, oa, ob, weights, *, tm, sub_rows):
    n, d = x2d.shape
    assert n % tm == 0 and tm % sub_rows == 0
    row = lambda width: pl.BlockSpec((tm, width), lambda i: (i, 0))
    return pl.pallas_call(
        functools.partial(_out_kernel, sub_rows=sub_rows),
        grid=(n // tm,),
        in_specs=[row(d), row(MIX_A), row(MIX_B)] + [_const_spec(w.shape) for w in weights],
        out_specs=row(d),
        out_shape=jax.ShapeDtypeStruct((n, d), F32),
        compiler_params=_compiler_params(("arbitrary",)),
        name="output_ffn",
    )(x2d, oa, ob, *weights)


def _rotate_half_cols(w):
    half = ROPE_DIM // 2
    return jnp.concatenate([-w[..., half:], w[..., :half]], axis=-1)


def _layout_weights(w_in, w_uq, w_uk, w_uv):
    d = w_in.shape[0]
    o = 0
    w_cq = w_in[:, o:o + Q_RANK]; o += Q_RANK
    w_ckv = w_in[:, o:o + KV_RANK]; o += KV_RANK
    w_kr = w_in[:, o:o + ROPE_DIM]; o += ROPE_DIM
    w_qb = w_in[:, o:o + MIX_B]; o += MIX_B
    w_kvb = w_in[:, o:]
    w1 = jnp.concatenate(
        [w_cq, w_ckv, jnp.tile(w_kr, (1, ROPE_PER_BLOCK)),
         jnp.tile(_rotate_half_cols(w_kr), (1, ROPE_PER_BLOCK)),
         w_qb, w_kvb], axis=1).astype(BF16)
    uq = w_uq.reshape(Q_RANK, H_A, NOPE_DIM + ROPE_DIM)
    uq_rope = uq[:, :, NOPE_DIM:]
    wq2 = jnp.concatenate(
        [uq[:, :, :NOPE_DIM].reshape(Q_RANK, H_A * NOPE_DIM),
         uq_rope.reshape(Q_RANK, H_A * ROPE_DIM),
         _rotate_half_cols(uq_rope).reshape(Q_RANK, H_A * ROPE_DIM)], axis=1).astype(BF16)
    uk = w_uk.reshape(KV_RANK, H_A * NOPE_DIM)
    uv = w_uv.reshape(KV_RANK, H_A * V_DIM)
    wkv2 = jnp.concatenate([uk, uv], axis=1).astype(BF16)
    return w1, wq2, wkv2, uk.T.astype(BF16), uv.astype(BF16)


def _rope_table(pos):
    inv = ROPE_THETA ** (-jnp.arange(0, ROPE_DIM, 2, dtype=F32) / ROPE_DIM)
    ang = pos.astype(F32)[:, None] * inv[None, :]
    cos = jnp.tile(jnp.cos(ang), (1, 2 * ROPE_PER_BLOCK))
    sin = jnp.tile(jnp.sin(ang), (1, 2 * ROPE_PER_BLOCK))
    return jnp.concatenate([cos, sin], axis=1)


def _row_tile(n, cap):
    tm = min(n, cap)
    assert n % tm == 0
    return tm


def kernel(x_prompt, x_sample, cache_mla_ckv, cache_mla_krope, cache_band_k, cache_band_v,
           w_in, g_attn, g_q, w_uq, g_kv, w_uk, w_uv, rel_bias, g_out_a, g_out_b, w_out,
           g_ffn, w_gate, w_up, w_down, g_final):
    depth = w_in.shape[0]
    assert depth == 1, "single-layer trunk"
    batch, seq, d = x_prompt.shape
    n_streams, n_tok, _ = x_sample.shape
    past = cache_mla_ckv.shape[2]

    w1, wq2, wkv2, wukt, wuv = _layout_weights(w_in[0], w_uq[0], w_uk[0], w_uv[0])
    proj_w = (g_attn, g_q, g_kv, w1, wq2, wkv2)
    out_w = (g_out_a, g_out_b, g_ffn, g_final[None, :], w_out[0].astype(BF16),
             w_gate[0].astype(BF16), w_up[0].astype(BF16), w_down[0].astype(BF16))
    table, table_k = _bias_table(rel_bias[0])

    n_keep = min(BAND_PAST, seq)
    xp = x_prompt.reshape(batch * seq, d)
    tm = _row_tile(n_keep, 512)
    step_rows = 2 * tm if (seq % (2 * tm) == 0 and n_keep == tm) else tm
    proj = _project(xp, _rope_table(jnp.arange(seq, dtype=jnp.int32)), proj_w,
                    seq_len=seq, n_keep=n_keep, tm=step_rows, sub_rows=tm, seq_minor=True)
    qn, qr, kn, krt, v, qb, kb, vb, ckv_p, krt_p, bkt_p, bvt_p = proj
    kr_p = jnp.swapaxes(krt_p, 1, 2)
    band_state = lambda s: jnp.transpose(s.reshape(batch, H_B, HD_B, n_keep), (0, 3, 1, 2))
    bk_p, bv_p = band_state(bkt_p), band_state(bvt_p)
    oa = _mla_prompt(qn, qr, kn, krt, v, seq_len=seq)
    ob = _band_prompt(qb, kb, vb, table_k, seq_len=seq)
    out_tm = 2 * tm if (batch * seq) % (2 * tm) == 0 else tm
    y_prompt = _output(xp, oa, ob, out_w, tm=out_tm, sub_rows=tm).reshape(batch, seq, d)

    n_s = n_streams * n_tok
    xs = x_sample.reshape(n_s, d)
    pos_s = past + jnp.tile(jnp.arange(n_tok, dtype=jnp.int32), n_streams)
    proj_s = _project(xs, _rope_table(pos_s), proj_w, seq_len=n_s, n_keep=n_s, tm=n_s,
                      sub_rows=n_s, seq_minor=False)
    oa_s, ob_s = _sample_attention(
        proj_s, (cache_mla_ckv[0], cache_mla_krope[0], cache_band_k[0], cache_band_v[0]),
        table, wukt, wuv, n_streams=n_streams, n_tok=n_tok)
    y_sample = _output(xs, oa_s, ob_s, out_w, tm=n_s, sub_rows=n_s).reshape(n_streams, n_tok, d)
    ckv_s, kr_s, bk_s, bv_s = proj_s[8:12]

    return (
        y_prompt, y_sample,
        ckv_p.reshape(1, batch, seq, KV_RANK), kr_p.reshape(1, batch, seq, ROPE_DIM),
        bk_p.reshape(1, batch, n_keep, H_B, HD_B), bv_p.reshape(1, batch, n_keep, H_B, HD_B),
        ckv_s.reshape(1, n_streams, n_tok, KV_RANK), kr_s.reshape(1, n_streams, n_tok, ROPE_DIM),
        bk_s.reshape(1, n_streams, n_tok, H_B, HD_B), bv_s.reshape(1, n_streams, n_tok, H_B, HD_B),
    )
```

```python
import functools

import jax
import jax.numpy as jnp
from jax import lax
from jax.experimental import pallas as pl
from jax.experimental.pallas import tpu as pltpu

CHUNK = 64
EPS = 1e-6
NEG_INF = -1e30
H_A = 8
NOPE_DIM = 64
ROPE_DIM = 32
V_DIM = 64
Q_RANK = 256
KV_RANK = 256
ROPE_THETA = 10000.0
MLA_SCALE = (NOPE_DIM + ROPE_DIM) ** -0.5
H_B = 8
HD_B = 64
N_PREV_CHUNKS = 8
BAND_PAST = N_PREV_CHUNKS * CHUNK
REL_CLIP = 256
BAND_SCALE = HD_B ** -0.5
LOG2E = 1.4426950408889634
MIX_A = H_A * V_DIM
MIX_B = H_B * HD_B

LANES = 128
MXU_DIM = 256
VMEM_LIMIT_BYTES = 56 * 1024 * 1024

ATT_TILE = 4 * CHUNK
BAND_TILES = N_PREV_CHUNKS * CHUNK // ATT_TILE + 1
BAND_Q = 2 * CHUNK
BAND_KEYS = N_PREV_CHUNKS * CHUNK + BAND_Q
ROPE_PER_BLOCK = LANES // ROPE_DIM

F32 = jnp.float32
BF16 = jnp.bfloat16


def _dot(a, b):
    return jnp.dot(a, b, preferred_element_type=F32)


def _dot_nt(a, b):
    return lax.dot_general(a, b, (((1,), (1,)), ((), ())), preferred_element_type=F32)


def _rms(x, g):
    return x * lax.rsqrt(jnp.mean(x * x, axis=-1, keepdims=True) + EPS) * g


def _lane_iota(width=LANES):
    return lax.broadcasted_iota(jnp.int32, (1, width), 1)


def _compiler_params(semantics):
    return pltpu.CompilerParams(dimension_semantics=semantics, vmem_limit_bytes=VMEM_LIMIT_BYTES)


def _const_spec(shape):
    nd = len(shape)
    return pl.BlockSpec(shape, lambda *_: (0,) * nd, pipeline_mode=pl.Buffered(1))


def _bias_kernel(g_ref, t_ref, tk_ref):
    rows, width = t_ref.shape[1], g_ref.shape[2]
    x = jnp.broadcast_to(g_ref[0], (rows, width))
    x = pltpu.roll(x, 0, 1, stride=1, stride_axis=0)
    x = x[:, width - t_ref.shape[2]:]
    qc = lax.broadcasted_iota(jnp.int32, x.shape, 0) // CHUNK
    kc = lax.broadcasted_iota(jnp.int32, x.shape, 1) // CHUNK
    visible = jnp.logical_and(kc >= qc, kc <= qc + N_PREV_CHUNKS)
    table = jnp.where(visible, x * LOG2E, NEG_INF)
    t_ref[0] = table
    tk_ref[0] = table.T


def _bias_table(rel_bias):
    n_keys = BAND_TILES * ATT_TILE
    width = n_keys + ATT_TILE
    n_const = width - 2 * REL_CLIP + 1
    g = jnp.concatenate(
        [jnp.broadcast_to(rel_bias[:, 2 * REL_CLIP:], (H_B, n_const)),
         rel_bias[:, 2 * REL_CLIP - 1:0:-1]], axis=1)
    g = g.reshape(H_B, 1, width)
    return pl.pallas_call(
        _bias_kernel,
        grid=(H_B,),
        in_specs=[pl.BlockSpec((1, 1, width), lambda h: (h, 0, 0))],
        out_specs=(pl.BlockSpec((1, ATT_TILE, n_keys), lambda h: (h, 0, 0)),
                   pl.BlockSpec((1, n_keys, ATT_TILE), lambda h: (h, 0, 0))),
        out_shape=(jax.ShapeDtypeStruct((H_B, ATT_TILE, n_keys), F32),
                   jax.ShapeDtypeStruct((H_B, n_keys, ATT_TILE), F32)),
        compiler_params=_compiler_params(("arbitrary",)),
        name="bias_table",
    )(g)


def _proj_kernel(x_ref, tab_ref, ga_ref, gq_ref, gkv_ref, w1_ref, wq2_ref, wkv2_ref,
                 qn_ref, qr_ref, kn_ref, krt_ref, v_ref, qb_ref, kb_ref, vb_ref,
                 ckv_ref, kr_ref, bk_ref, bv_ref, *, sub_rows, steps_per_seq, keep_from_sub,
                 seq_minor):
    n_nope = H_A * NOPE_DIM
    n_rope = H_A * ROPE_DIM
    o_kv = Q_RANK
    o_kr = o_kv + KV_RANK
    o_b = o_kr + 2 * LANES
    n_sub = x_ref.shape[0] // sub_rows
    tiles = [slice(r * sub_rows, (r + 1) * sub_rows) for r in range(n_sub)]
    step_in_seq = pl.program_id(0) % steps_per_seq

    def normed(rows):
        return _rms(x_ref[rows, :], ga_ref[...]).astype(BF16)

    def first_dots(xn):
        return (_dot(xn, w1_ref[:, 0:Q_RANK]),
                _dot(xn, w1_ref[:, o_kv:o_kv + KV_RANK]),
                _dot(xn, w1_ref[:, o_kr:o_kr + 2 * LANES]),
                _dot(xn, w1_ref[:, o_b:o_b + 3 * MIX_B]))

    def finish(r, rows, c_q, c_kv, krr, qkv):
        cos = tab_ref[rows, 0:LANES]
        sin = tab_ref[rows, LANES:2 * LANES]

        qb_ref[rows, :] = (qkv[:, 0:MIX_B] * (BAND_SCALE * LOG2E)).astype(BF16)
        kb = qkv[:, MIX_B:2 * MIX_B]
        vb = qkv[:, 2 * MIX_B:]
        kb_ref[rows, :] = kb.astype(BF16)
        vb_ref[rows, :] = vb.astype(BF16)

        cqn = _rms(c_q, gq_ref[...]).astype(BF16)
        q2 = _dot(cqn, wq2_ref[...])
        qn_ref[rows, :] = (q2[:, 0:n_nope] * (MLA_SCALE * LOG2E)).astype(BF16)
        cos2 = jnp.concatenate([cos] * (n_rope // LANES), axis=1)
        sin2 = jnp.concatenate([sin] * (n_rope // LANES), axis=1)
        q_rope = q2[:, n_nope:n_nope + n_rope] * cos2 + q2[:, n_nope + n_rope:] * sin2
        qr_ref[rows, :] = (q_rope * (MLA_SCALE * LOG2E)).astype(BF16)

        ckvn = _rms(c_kv, gkv_ref[...])
        ckv_ref[rows, :] = ckvn
        kv2 = _dot(ckvn.astype(BF16), wkv2_ref[...])
        kn_ref[rows, :] = kv2[:, 0:n_nope].astype(BF16)
        v_ref[rows, :] = kv2[:, n_nope:].astype(BF16)

        k_rope = krr[:, 0:LANES] * cos + krr[:, LANES:] * sin
        krt_ref[rows, :] = k_rope.astype(BF16)
        if seq_minor:
            kr_ref[0, :, rows] = k_rope.T[0:ROPE_DIM, :]
        else:
            kr_ref[rows, :] = k_rope[:, 0:ROPE_DIM]

        if n_sub == 1 or r == n_sub - 1:
            @pl.when(step_in_seq * n_sub + r >= keep_from_sub)
            def _():
                bk_ref[0] = kb.T if seq_minor else kb
                bv_ref[0] = vb.T if seq_minor else vb

    xn = normed(tiles[0])
    for r, rows in enumerate(tiles):
        heads = first_dots(xn)
        if r + 1 < n_sub:
            xn = normed(tiles[r + 1])
        finish(r, rows, *heads)


def _project(x2d, tab, weights, *, seq_len, n_keep, tm, sub_rows, seq_minor):
    n, d = x2d.shape
    g_attn, g_q, g_kv, w1, wq2, wkv2 = weights
    assert n % seq_len == 0 and seq_len % tm == 0 and tm % sub_rows == 0 and n_keep % sub_rows == 0
    n_sub = tm // sub_rows
    assert n_sub == 1 or n_keep == sub_rows, "several row tiles per step: one kept tile per sequence"
    steps_per_seq = seq_len // tm
    keep_from_sub = (seq_len - n_keep) // sub_rows
    n_seq = n // seq_len
    keep_rows = n_keep

    row = lambda width: pl.BlockSpec((tm, width), lambda i: (i, 0))
    keep_tile = lambda i: jnp.maximum((i % steps_per_seq) * n_sub + n_sub - 1 - keep_from_sub, 0)
    if seq_minor:
        kr_spec = pl.BlockSpec((1, ROPE_DIM, tm), lambda i: (i // steps_per_seq, 0, i % steps_per_seq))
        kr_shape = jax.ShapeDtypeStruct((n_seq, ROPE_DIM, seq_len), F32)
        keep_spec = pl.BlockSpec((1, MIX_B, sub_rows), lambda i: (i // steps_per_seq, 0, keep_tile(i)))
        keep_shape = jax.ShapeDtypeStruct((n_seq, MIX_B, keep_rows), F32)
    else:
        kr_spec = row(ROPE_DIM)
        kr_shape = jax.ShapeDtypeStruct((n, ROPE_DIM), F32)
        keep_spec = pl.BlockSpec((1, sub_rows, MIX_B), lambda i: (i // steps_per_seq, keep_tile(i), 0))
        keep_shape = jax.ShapeDtypeStruct((n_seq, keep_rows, MIX_B), F32)
    bf = lambda width: jax.ShapeDtypeStruct((n, width), BF16)
    out_shape = (
        bf(H_A * NOPE_DIM), bf(H_A * ROPE_DIM), bf(H_A * NOPE_DIM), bf(LANES), bf(MIX_A),
        bf(MIX_B), bf(MIX_B), bf(MIX_B),
        jax.ShapeDtypeStruct((n, KV_RANK), F32), kr_shape, keep_shape, keep_shape,
    )
    out_specs = (
        row(H_A * NOPE_DIM), row(H_A * ROPE_DIM), row(H_A * NOPE_DIM), row(LANES), row(MIX_A),
        row(MIX_B), row(MIX_B), row(MIX_B), row(KV_RANK), kr_spec, keep_spec, keep_spec,
    )
    in_specs = [
        row(d),
        pl.BlockSpec((tm, 2 * LANES), lambda i: (i % steps_per_seq, 0)),
        _const_spec(g_attn.shape), _const_spec(g_q.shape), _const_spec(g_kv.shape),
        _const_spec(w1.shape), _const_spec(wq2.shape), _const_spec(wkv2.shape),
    ]
    return pl.pallas_call(
        functools.partial(_proj_kernel, sub_rows=sub_rows, steps_per_seq=steps_per_seq,
                          keep_from_sub=keep_from_sub, seq_minor=seq_minor),
        grid=(n // tm,),
        in_specs=in_specs,
        out_specs=out_specs,
        out_shape=out_shape,
        compiler_params=_compiler_params(("arbitrary",)),
        name="projection",
    )(x2d, tab, g_attn, g_q, g_kv, w1, wq2, wkv2)


def _pipelined_softmax(tiles, scores, attend, lookahead):
    ahead = [scores(tile) for tile in tiles[:lookahead]]
    pending = None
    for i, tile in enumerate(tiles):
        s = ahead.pop(0)
        if i + lookahead < len(tiles):
            ahead.append(scores(tiles[i + lookahead]))
        p = jnp.exp2(s - jnp.max(s, axis=0, keepdims=True))
        sums = jnp.sum(p, axis=0, keepdims=True)
        if pending is not None:
            attend(*pending)
        pending = (tile, p.astype(BF16), sums)
    attend(*pending)


def _pair_output(acc, sums, t):
    half = acc.shape[0] // 2
    o_t = jnp.concatenate([acc[:half, :t] / sums[:, :t], acc[half:, t:] / sums[:, t:]], axis=0)
    return o_t.T


def _mla_kernel(qn_ref, qr_ref, kn_ref, krt_ref, v_ref, o_ref, ve_ref, vo_ref, *, n_tiles):
    t = ATT_TILE
    n_pairs = ve_ref.shape[0]
    lane = _lane_iota()
    low = lane < V_DIM
    own_even = lax.broadcasted_iota(jnp.int32, (LANES, 1), 0) < V_DIM
    for pr in range(n_pairs):
        vt = v_ref[:, pr * LANES:(pr + 1) * LANES].astype(F32).T
        ve_ref[pr] = jnp.where(own_even, vt, 1.0).astype(BF16)
        vo_ref[pr] = jnp.where(own_even, 1.0, vt).astype(BF16)
    key_chunk = lax.broadcasted_iota(jnp.int32, (t, 2 * t), 0) // CHUNK
    query_chunk = (lax.broadcasted_iota(jnp.int32, (t, 2 * t), 1) % t) // CHUNK
    causal = key_chunk <= query_chunk

    def scores(unit):
        pr, qi = unit
        cols = slice(pr * LANES, (pr + 1) * LANES)
        r0, n_keys = qi * t, (qi + 1) * t
        qn = qn_ref[r0:r0 + t, cols]
        qr = qr_ref[r0:r0 + t, :]
        k2 = jnp.concatenate([kn_ref[0:n_keys, cols], krt_ref[0:n_keys, :]], axis=1)
        q_heads = []
        for hh in range(2):
            nope_mask = low if hh == 0 else jnp.logical_not(low)
            lo = (2 * pr + hh) * ROPE_DIM
            rope_mask = jnp.logical_and(lane >= lo, lane < lo + ROPE_DIM)
            q_heads.append(jnp.concatenate(
                [jnp.where(nope_mask, qn, jnp.zeros_like(qn)),
                 jnp.where(rope_mask, qr, jnp.zeros_like(qr))], axis=1))
        s = _dot_nt(k2, jnp.concatenate(q_heads, axis=0))
        s_diag = jnp.where(causal, s[r0:, :], NEG_INF)
        return s_diag if qi == 0 else jnp.concatenate([s[:r0, :], s_diag], axis=0)

    def attend(unit, p, _):
        pr, qi = unit
        r0, n_keys = qi * t, (qi + 1) * t
        acc_even = _dot(ve_ref[pr, :, 0:n_keys], p[:, :t])
        acc_odd = _dot(vo_ref[pr, :, 0:n_keys], p[:, t:])
        num = jnp.where(own_even, acc_even, acc_odd)
        sums = jnp.where(own_even, acc_odd, acc_even)
        den = jnp.concatenate([sums[V_DIM:], sums[:V_DIM]], axis=0)
        o_ref[r0:r0 + t, pr * LANES:(pr + 1) * LANES] = (num / den).T

    units = [(pr, qi) for qi in range(n_tiles) for pr in range(n_pairs)]
    _pipelined_softmax(units, scores, attend, lookahead=1)


def _mla_prompt(qn, qr, kn, krt, v, *, seq_len):
    n = qn.shape[0]
    assert seq_len % ATT_TILE == 0
    n_pairs = ROPE_PER_BLOCK // 2
    width = n_pairs * LANES
    blk = lambda w, f: pl.BlockSpec((seq_len, w), f)
    return pl.pallas_call(
        functools.partial(_mla_kernel, n_tiles=seq_len // ATT_TILE),
        grid=(n // seq_len, H_A // (2 * n_pairs)),
        in_specs=[
            blk(width, lambda b, j: (b, j)),
            blk(LANES, lambda b, j: (b, j)),
            blk(width, lambda b, j: (b, j)),
            blk(LANES, lambda b, j: (b, 0)),
            blk(width, lambda b, j: (b, j)),
        ],
        out_specs=blk(width, lambda b, j: (b, j)),
        out_shape=jax.ShapeDtypeStruct((n, MIX_A), F32),
        scratch_shapes=[pltpu.VMEM((n_pairs, LANES, seq_len), BF16),
                        pltpu.VMEM((n_pairs, LANES, seq_len), BF16)],
        compiler_params=_compiler_params(("arbitrary", "arbitrary")),
        name="mla_prompt",
    )(qn, qr, kn, krt, v)


def _band_kernel(q_ref, k_ref, v_ref, t_ref, o_ref, vt_ref, *, n_tiles):
    t = BAND_Q
    n_pairs = vt_ref.shape[0]
    low = _lane_iota() < HD_B
    for pr in range(n_pairs):
        vt_ref[pr] = v_ref[:, pr * LANES:(pr + 1) * LANES].astype(F32).T.astype(BF16)

    def window(g):
        return max((g + 1) * t - BAND_KEYS, 0), (g + 1) * t

    def scores(unit):
        pr, g = unit
        cols = slice(pr * LANES, (pr + 1) * LANES)
        k_lo, k_hi = window(g)
        q = q_ref[g * t:(g + 1) * t, cols]
        zero_q = jnp.zeros_like(q)
        q2 = jnp.concatenate([jnp.where(low, q, zero_q), jnp.where(low, zero_q, q)], axis=0)
        first = BAND_KEYS - (k_hi - k_lo)
        bias = jnp.concatenate([t_ref[2 * pr, first:BAND_KEYS, 0:t],
                                t_ref[2 * pr + 1, first:BAND_KEYS, 0:t]], axis=1)
        return _dot_nt(k_ref[k_lo:k_hi, cols], q2) + bias

    def attend(unit, p, sums):
        pr, g = unit
        k_lo, k_hi = window(g)
        o_ref[g * t:(g + 1) * t, pr * LANES:(pr + 1) * LANES] = _pair_output(
            _dot(vt_ref[pr, :, k_lo:k_hi], p), sums, t)

    units = [(pr, g) for g in range(n_tiles) for pr in range(n_pairs)]
    _pipelined_softmax(units, scores, attend, lookahead=2)


def _band_prompt(qb, kb, vb, table, *, seq_len):
    n = qb.shape[0]
    assert seq_len % BAND_Q == 0
    n_pairs = 2
    blk = pl.BlockSpec((seq_len, n_pairs * LANES), lambda b, j: (b, j))
    return pl.pallas_call(
        functools.partial(_band_kernel, n_tiles=seq_len // BAND_Q),
        grid=(n // seq_len, H_B // (2 * n_pairs)),
        in_specs=[blk, blk, blk,
                  pl.BlockSpec((2 * n_pairs,) + table.shape[1:], lambda b, j: (j, 0, 0))],
        out_specs=blk,
        out_shape=jax.ShapeDtypeStruct((n, MIX_B), F32),
        scratch_shapes=[pltpu.VMEM((n_pairs, LANES, seq_len), BF16)],
        compiler_params=_compiler_params(("arbitrary", "arbitrary")),
        name="band_prompt",
    )(qb, kb, vb, table)


def _sample_kernel(qn_ref, qr_ref, ckvn_ref, krtn_ref, qb_ref, kbn_ref, vbn_ref,
                   cckv_ref, ckrt_ref, cbkt_ref, cbvt_ref, t_ref, wukt_ref, wuv_ref,
                   oa_ref, ob_ref, *, n_tok):
    lane = _lane_iota()
    low = lane < V_DIM

    qn = qn_ref[...]
    qr = qr_ref[...]
    q_lat, q_rope = [], []
    for h in range(H_A):
        pair, par = divmod(h, 2)
        blk = qn[:, pair * LANES:(pair + 1) * LANES]
        head_mask = low if par == 0 else jnp.logical_not(low)
        q_h = jnp.where(head_mask, blk, jnp.zeros_like(blk))
        q_lat.append(_dot(q_h, wukt_ref[pair * LANES:(pair + 1) * LANES, :]))
        rblk, rpos = divmod(h, ROPE_PER_BLOCK)
        r = qr[:, rblk * LANES:(rblk + 1) * LANES]
        own = jnp.logical_and(lane >= rpos * ROPE_DIM, lane < (rpos + 1) * ROPE_DIM)
        q_rope.append(jnp.where(own, r, jnp.zeros_like(r)))
    q_lat = jnp.concatenate(q_lat, axis=0).astype(BF16)
    q_rope = jnp.concatenate(q_rope, axis=0)

    ckv_c = cckv_ref[0].astype(BF16)
    krt_c = jnp.concatenate([ckrt_ref[0].astype(BF16)] * ROPE_PER_BLOCK, axis=0)
    ckv_n = ckvn_ref[...].astype(BF16)
    s_c = _dot_nt(q_lat, ckv_c) + _dot(q_rope, krt_c)
    s_n = _dot_nt(q_lat, ckv_n) + _dot_nt(q_rope, krtn_ref[...])
    m = jnp.maximum(jnp.max(s_c, axis=1, keepdims=True), jnp.max(s_n, axis=1, keepdims=True))
    p_c = jnp.exp2(s_c - m)
    p_n = jnp.exp2(s_n - m)
    l = jnp.sum(p_c, axis=1, keepdims=True) + jnp.sum(p_n, axis=1, keepdims=True)
    o_lat = ((_dot(p_c.astype(BF16), ckv_c) + _dot(p_n.astype(BF16), ckv_n)) / l).astype(BF16)
    wuv = wuv_ref[...]
    col_head = lax.broadcasted_iota(jnp.int32, (1, MIX_A), 1) // V_DIM
    out_a = None
    for h in range(H_A):
        w_h = jnp.where(col_head == h, wuv, jnp.zeros_like(wuv))
        o_h = _dot(o_lat[h * n_tok:(h + 1) * n_tok], w_h)
        out_a = o_h if out_a is None else out_a + o_h
    oa_ref[...] = out_a

    n_cache = cbkt_ref.shape[2]
    qb = qb_ref[...]
    kb_n = kbn_ref[...]
    vb_n = vbn_ref[...]
    band_off = BAND_PAST - n_cache
    out_pairs = []
    for pair in range(H_B // 2):
        cols = slice(pair * LANES, (pair + 1) * LANES)
        blk = qb[:, cols]
        q2 = jnp.concatenate([jnp.where(low, blk, jnp.zeros_like(blk)),
                              jnp.where(low, jnp.zeros_like(blk), blk)], axis=0)
        bias = jnp.concatenate([t_ref[2 * pair], t_ref[2 * pair + 1]], axis=0)
        kt_c = cbkt_ref[0, cols, :].astype(BF16)
        vt_c = cbvt_ref[0, cols, :].astype(BF16)
        s_c = _dot(q2, kt_c) + bias[:, band_off:BAND_PAST]
        s_n = _dot_nt(q2, kb_n[:, cols]) + bias[:, BAND_PAST:BAND_PAST + n_tok]
        m = jnp.maximum(jnp.max(s_c, axis=1, keepdims=True), jnp.max(s_n, axis=1, keepdims=True))
        p_c = jnp.exp2(s_c - m)
        p_n = jnp.exp2(s_n - m)
        l = jnp.sum(p_c, axis=1, keepdims=True) + jnp.sum(p_n, axis=1, keepdims=True)
        o2 = (_dot_nt(p_c.astype(BF16), vt_c) + _dot(p_n.astype(BF16), vb_n[:, cols])) / l
        out_pairs.append(jnp.where(low, o2[0:n_tok], o2[n_tok:2 * n_tok]))
    ob_ref[...] = jnp.concatenate(out_pairs, axis=1)


def _sample_attention(proj, caches, table, wukt, wuv, *, n_streams, n_tok):
    qn, qr, _, krt, _, qb, kb, vb, ckv, _, _, _ = proj
    cache_ckv, cache_kr, cache_bk, cache_bv = caches
    n_band = cache_bk.shape[1]
    assert n_band <= BAND_PAST and BAND_PAST + n_tok <= table.shape[2]
    ckrt = jnp.swapaxes(cache_kr, 1, 2)
    cbkt = jnp.transpose(cache_bk, (0, 2, 3, 1)).reshape(n_streams, MIX_B, n_band)
    cbvt = jnp.transpose(cache_bv, (0, 2, 3, 1)).reshape(n_streams, MIX_B, n_band)
    tok = lambda width: pl.BlockSpec((n_tok, width), lambda b: (b, 0))
    per_stream = lambda a: pl.BlockSpec((1,) + a.shape[1:], lambda b: (b, 0, 0))
    out = jax.ShapeDtypeStruct((n_streams * n_tok, MIX_A), F32)
    return pl.pallas_call(
        functools.partial(_sample_kernel, n_tok=n_tok),
        grid=(n_streams,),
        in_specs=[
            tok(qn.shape[1]), tok(qr.shape[1]), tok(ckv.shape[1]), tok(krt.shape[1]),
            tok(qb.shape[1]), tok(kb.shape[1]), tok(vb.shape[1]),
            per_stream(cache_ckv), per_stream(ckrt), per_stream(cbkt), per_stream(cbvt),
            pl.BlockSpec((H_B, n_tok, table.shape[2]), lambda b: (0, 0, 0)),
            _const_spec(wukt.shape), _const_spec(wuv.shape),
        ],
        out_specs=(tok(MIX_A), tok(MIX_B)),
        out_shape=(out, out),
        compiler_params=_compiler_params(("arbitrary",)),
        name="sample_attention",
    )(qn, qr, ckv, krt, qb, kb, vb, cache_ckv, ckrt, cbkt, cbvt, table, wukt, wuv)


def _ffn_chunks(d_ff):
    chunks, start = [], 0
    while start < d_ff:
        size = min(2 * MXU_DIM, d_ff - start)
        chunks.append((start, size))
        start += size
    return chunks


def _out_kernel(x_ref, oa_ref, ob_ref, goa_ref, gob_ref, gffn_ref, gfin_ref,
                wout_ref, wg_ref, wu_ref, wd_ref, y_ref, *, sub_rows):
    chunks = _ffn_chunks(wg_ref.shape[1])

    def mixed(rows):
        mix = jnp.concatenate([_rms(oa_ref[rows, :], goa_ref[...]), _rms(ob_ref[rows, :], gob_ref[...])],
                              axis=1).astype(BF16)
        x1 = x_ref[rows, :] + _dot(mix, wout_ref[...])
        return x1, _rms(x1, gffn_ref[...]).astype(BF16)

    def ffn_part(h, chunk):
        start, size = chunk
        gate = _dot(h, wg_ref[:, start:start + size])
        up = _dot(h, wu_ref[:, start:start + size])
        act = (jax.nn.silu(gate) * up).astype(BF16)
        return _dot(act, wd_ref[start:start + size, :])

    n_sub = x_ref.shape[0] // sub_rows
    tiles = [slice(r * sub_rows, (r + 1) * sub_rows) for r in range(n_sub)]
    state = mixed(tiles[0])
    for r, rows in enumerate(tiles):
        x1, h = state
        ffn = ffn_part(h, chunks[0])
        if r + 1 < n_sub:
            state = mixed(tiles[r + 1])
        for chunk in chunks[1:]:
            ffn = ffn + ffn_part(h, chunk)
        y_ref[rows, :] = _rms(x1 + ffn, gfin_ref[...])


def _output(x2d, oa, ob, weights, *, tm, sub_rows):
    n, d = x2d.shape
    assert n % tm == 0 and tm % sub_rows == 0
    row = lambda width: pl.BlockSpec((tm, width), lambda i: (i, 0))
    return pl.pallas_call(
        functools.partial(_out_kernel, sub_rows=sub_rows),
        grid=(n // tm,),
        in_specs=[row(d), row(MIX_A), row(MIX_B)] + [_const_spec(w.shape) for w in weights],
        out_specs=row(d),
        out_shape=jax.ShapeDtypeStruct((n, d), F32),
        compiler_params=_compiler_params(("arbitrary",)),
        name="output_ffn",
    )(x2d, oa, ob, *weights)


def _rotate_half_cols(w):
    half = ROPE_DIM // 2
    return jnp.concatenate([-w[..., half:], w[..., :half]], axis=-1)


def _layout_weights(w_in, w_uq, w_uk, w_uv):
    d = w_in.shape[0]
    o = 0
    w_cq = w_in[:, o:o + Q_RANK]; o += Q_RANK
    w_ckv = w_in[:, o:o + KV_RANK]; o += KV_RANK
    w_kr = w_in[:, o:o + ROPE_DIM]; o += ROPE_DIM
    w_qb = w_in[:, o:o + MIX_B]; o += MIX_B
    w_kvb = w_in[:, o:]
    w1 = jnp.concatenate(
        [w_cq, w_ckv, jnp.tile(w_kr, (1, ROPE_PER_BLOCK)),
         jnp.tile(_rotate_half_cols(w_kr), (1, ROPE_PER_BLOCK)),
         w_qb, w_kvb], axis=1).astype(BF16)
    uq = w_uq.reshape(Q_RANK, H_A, NOPE_DIM + ROPE_DIM)
    uq_rope = uq[:, :, NOPE_DIM:]
    wq2 = jnp.concatenate(
        [uq[:, :, :NOPE_DIM].reshape(Q_RANK, H_A * NOPE_DIM),
         uq_rope.reshape(Q_RANK, H_A * ROPE_DIM),
         _rotate_half_cols(uq_rope).reshape(Q_RANK, H_A * ROPE_DIM)], axis=1).astype(BF16)
    uk = w_uk.reshape(KV_RANK, H_A * NOPE_DIM)
    uv = w_uv.reshape(KV_RANK, H_A * V_DIM)
    wkv2 = jnp.concatenate([uk, uv], axis=1).astype(BF16)
    return w1, wq2, wkv2, uk.T.astype(BF16), uv.astype(BF16)


def _rope_table(pos):
    inv = ROPE_THETA ** (-jnp.arange(0, ROPE_DIM, 2, dtype=F32) / ROPE_DIM)
    ang = pos.astype(F32)[:, None] * inv[None, :]
    cos = jnp.tile(jnp.cos(ang), (1, 2 * ROPE_PER_BLOCK))
    sin = jnp.tile(jnp.sin(ang), (1, 2 * ROPE_PER_BLOCK))
    return jnp.concatenate([cos, sin], axis=1)


def _row_tile(n, cap):
    tm = min(n, cap)
    assert n % tm == 0
    return tm


def kernel(x_prompt, x_sample, cache_mla_ckv, cache_mla_krope, cache_band_k, cache_band_v,
           w_in, g_attn, g_q, w_uq, g_kv, w_uk, w_uv, rel_bias, g_out_a, g_out_b, w_out,
           g_ffn, w_gate, w_up, w_down, g_final):
    depth = w_in.shape[0]
    assert depth == 1, "single-layer trunk"
    batch, seq, d = x_prompt.shape
    n_streams, n_tok, _ = x_sample.shape
    past = cache_mla_ckv.shape[2]

    w1, wq2, wkv2, wukt, wuv = _layout_weights(w_in[0], w_uq[0], w_uk[0], w_uv[0])
    proj_w = (g_attn, g_q, g_kv, w1, wq2, wkv2)
    out_w = (g_out_a, g_out_b, g_ffn, g_final[None, :], w_out[0].astype(BF16),
             w_gate[0].astype(BF16), w_up[0].astype(BF16), w_down[0].astype(BF16))
    table, table_k = _bias_table(rel_bias[0])

    n_keep = min(BAND_PAST, seq)
    xp = x_prompt.reshape(batch * seq, d)
    tm = _row_tile(n_keep, 512)
    step_rows = 2 * tm if (seq % (2 * tm) == 0 and n_keep == tm) else tm
    proj = _project(xp, _rope_table(jnp.arange(seq, dtype=jnp.int32)), proj_w,
                    seq_len=seq, n_keep=n_keep, tm=step_rows, sub_rows=tm, seq_minor=True)
    qn, qr, kn, krt, v, qb, kb, vb, ckv_p, krt_p, bkt_p, bvt_p = proj
    kr_p = jnp.swapaxes(krt_p, 1, 2)
    band_state = lambda s: jnp.transpose(s.reshape(batch, H_B, HD_B, n_keep), (0, 3, 1, 2))
    bk_p, bv_p = band_state(bkt_p), band_state(bvt_p)
    oa = _mla_prompt(qn, qr, kn, krt, v, seq_len=seq)
    ob = _band_prompt(qb, kb, vb, table_k, seq_len=seq)
    out_tm = 2 * tm if (batch * seq) % (2 * tm) == 0 else tm
    y_prompt = _output(xp, oa, ob, out_w, tm=out_tm, sub_rows=tm).reshape(batch, seq, d)

    n_s = n_streams * n_tok
    xs = x_sample.reshape(n_s, d)
    pos_s = past + jnp.tile(jnp.arange(n_tok, dtype=jnp.int32), n_streams)
    proj_s = _project(xs, _rope_table(pos_s), proj_w, seq_len=n_s, n_keep=n_s, tm=n_s,
                      sub_rows=n_s, seq_minor=False)
    oa_s, ob_s = _sample_attention(
        proj_s, (cache_mla_ckv[0], cache_mla_krope[0], cache_band_k[0], cache_band_v[0]),
        table, wukt, wuv, n_streams=n_streams, n_tok=n_tok)
    y_sample = _output(xs, oa_s, ob_s, out_w, tm=n_s, sub_rows=n_s).reshape(n_streams, n_tok, d)
    ckv_s, kr_s, bk_s, bv_s = proj_s[8:12]

    return (
        y_prompt, y_sample,
        ckv_p.reshape(1, batch, seq, KV_RANK), kr_p.reshape(1, batch, seq, ROPE_DIM),
        bk_p.reshape(1, batch, n_keep, H_B, HD_B), bv_p.reshape(1, batch, n_keep, H_B, HD_B),
        ckv_s.reshape(1, n_streams, n_tok, KV_RANK), kr_s.reshape(1, n_streams, n_tok, ROPE_DIM),
        bk_s.reshape(1, n_streams, n_tok, H_B, HD_B), bv_s.reshape(1, n_streams, n_tok, H_B, HD_B),
    )
```
